```python
import jax, jax.numpy as jnp
from jax import lax
import numpy as np

D_MODEL = 1024
BATCH = 2
SEQ = 8192
DEPTH = 1
DEC_BATCH = 32
DEC_SEQ = 8
PAST_LEN = 8192
PAGE_SIZE = 128

D_CONV = D_MODEL // 2
CONV_WIDTH = 3
N_HEADS = 8
HEAD_DIM = 64
D_ATT = N_HEADS * HEAD_DIM
D_FF = 4 * D_MODEL
Q_BLOCK = 128
RMS_EPS = 1e-6
SB_BIAS_INIT = -8.0
SPLIT_POINTS = (D_CONV, 2 * D_CONV, 3 * D_CONV,
                3 * D_CONV + D_ATT, 3 * D_CONV + 2 * D_ATT, 3 * D_CONV + 3 * D_ATT,
                3 * D_CONV + 3 * D_ATT + D_MODEL)
D_IN = 3 * D_CONV + 3 * D_ATT + 2 * D_MODEL

kernel_name = "hybrid_shortconv_stickbreaking_step"


def rmsnorm(x, w):
    xf = x.astype(jnp.float32)
    y = xf * lax.rsqrt(jnp.mean(xf * xf, axis=-1, keepdims=True) + RMS_EPS)
    return (y * w.astype(jnp.float32)).astype(x.dtype)


def stick_breaking_block(q, k, v, bias, q_pos, k_pos):
    z = (jnp.einsum('bhqd,bhkd->bhqk', q, k).astype(jnp.float32) * (HEAD_DIM ** -0.5)
         + bias.astype(jnp.float32)[None, :, None, None])
    causal = k_pos[None, :] < q_pos[:, None]
    log_beta = jax.nn.log_sigmoid(z)
    log_stay = jnp.where(causal, log_beta - z, 0.0)
    log_after = lax.cumsum(log_stay, axis=3, reverse=True) - log_stay
    w = jnp.where(causal, jnp.exp(log_beta + log_after), 0.0)
    return jnp.einsum('bhqk,bhkd->bhqd', w.astype(v.dtype), v)


def sb_prompt(q, k, v, bias):
    B, S = q.shape[0], q.shape[1]
    nb = S // Q_BLOCK
    qb = q.reshape(B, nb, Q_BLOCK, N_HEADS, HEAD_DIM).transpose(1, 0, 3, 2, 4)
    kt = k.transpose(0, 2, 1, 3)
    vt = v.transpose(0, 2, 1, 3)
    k_pos = jnp.arange(S)

    def one_block(args):
        q_blk, i = args
        q_pos = i * Q_BLOCK + jnp.arange(Q_BLOCK)
        return stick_breaking_block(q_blk, kt, vt, bias, q_pos, k_pos)

    o = lax.map(one_block, (qb, jnp.arange(nb)))
    return o.transpose(1, 0, 3, 2, 4).reshape(B, S, D_ATT)


def sb_sample(q, k_new, v_new, k_past, v_past, bias):
    B, L = q.shape[0], q.shape[1]
    P = k_past.shape[1]
    kt = jnp.concatenate([k_past, k_new], axis=1).transpose(0, 2, 1, 3)
    vt = jnp.concatenate([v_past, v_new], axis=1).transpose(0, 2, 1, 3)
    q_pos = P + jnp.arange(L)
    k_pos = jnp.arange(P + L)
    o = stick_breaking_block(q.transpose(0, 2, 1, 3), kt, vt, bias, q_pos, k_pos)
    return o.transpose(0, 2, 1, 3).reshape(B, L, D_ATT)


def hybrid_layer(x, conv_prev, k_past, v_past, norm1_w, w_in, conv_w, w_proj_conv,
                 w_proj_attn, att_bias, w_out, norm2_w, w_up, w_down):
    B, L = x.shape[0], x.shape[1]
    h = rmsnorm(x, norm1_w)
    proj = h @ w_in
    b_g, c_g, u, q, k, v, gate_conv, gate_att = jnp.split(proj, SPLIT_POINTS, axis=-1)
    cu = c_g * u
    cu_full = jnp.concatenate([conv_prev, cu], axis=1)
    conv = sum(conv_w[j] * cu_full[:, j:j + L] for j in range(CONV_WIDTH))
    y_conv = (b_g * conv) @ w_proj_conv
    new_conv = cu_full[:, L:]
    q = q.reshape(B, L, N_HEADS, HEAD_DIM)
    k = k.reshape(B, L, N_HEADS, HEAD_DIM)
    v = v.reshape(B, L, N_HEADS, HEAD_DIM)
    if k_past is None:
        o = sb_prompt(q, k, v, att_bias)
    else:
        o = sb_sample(q, k, v, k_past, v_past, att_bias)
    y_att = o @ w_proj_attn
    merged = jax.nn.sigmoid(gate_conv) * y_conv + jax.nn.sigmoid(gate_att) * y_att
    x = x + merged @ w_out
    f = rmsnorm(x, norm2_w) @ w_up
    x = x + jnp.square(jax.nn.relu(f)) @ w_down
    return x, new_conv, k, v


def setup_inputs(seed: int = 0) -> dict:
    key = jax.random.key(seed)
    ks = jax.random.split(key, 20)
    n_pages = PAST_LEN // PAGE_SIZE
    n_used = DEC_BATCH * n_pages
    n_pool = n_used + n_used // 4
    f32 = jnp.float32
    x_prompt = jax.random.normal(ks[0], (BATCH, SEQ, D_MODEL), f32)
    x_sample = jax.random.normal(ks[1], (DEC_BATCH, DEC_SEQ, D_MODEL), f32)
    cache_k = jax.random.normal(ks[2], (DEPTH, n_pool, PAGE_SIZE, N_HEADS, HEAD_DIM), f32)
    cache_v = jax.random.normal(ks[3], (DEPTH, n_pool, PAGE_SIZE, N_HEADS, HEAD_DIM), f32)
    state_conv = jax.random.normal(ks[4], (DEPTH, DEC_BATCH, CONV_WIDTH - 1, D_CONV), f32)
    page_table = jax.random.permutation(ks[5], n_pool)[:n_used].astype(jnp.int32).reshape(DEC_BATCH, n_pages)
    norm1_w = 1.0 + 0.02 * jax.random.normal(ks[6], (DEPTH, D_MODEL), f32)
    w_in = jax.random.normal(ks[7], (DEPTH, D_MODEL, D_IN), f32) * D_MODEL ** -0.5
    conv_w = jax.random.normal(ks[8], (DEPTH, CONV_WIDTH, D_CONV), f32) * CONV_WIDTH ** -0.5
    w_proj_conv = jax.random.normal(ks[9], (DEPTH, D_CONV, D_MODEL), f32) * D_CONV ** -0.5
    w_proj_attn = jax.random.normal(ks[10], (DEPTH, D_ATT, D_MODEL), f32) * D_ATT ** -0.5
    att_bias = SB_BIAS_INIT + 0.1 * jax.random.normal(ks[16], (DEPTH, N_HEADS), f32)
    w_out = jax.random.normal(ks[11], (DEPTH, D_MODEL, D_MODEL), f32) * D_MODEL ** -0.5
    norm2_w = 1.0 + 0.02 * jax.random.normal(ks[12], (DEPTH, D_MODEL), f32)
    w_up = jax.random.normal(ks[13], (DEPTH, D_MODEL, D_FF), f32) * D_MODEL ** -0.5
    w_down = jax.random.normal(ks[14], (DEPTH, D_FF, D_MODEL), f32) * D_FF ** -0.5
    norm_f_w = 1.0 + 0.02 * jax.random.normal(ks[15], (D_MODEL,), f32)
    return {"x_prompt": x_prompt, "x_sample": x_sample, "cache_k": cache_k, "cache_v": cache_v,
            "state_conv": state_conv, "page_table": page_table, "norm1_w": norm1_w, "w_in": w_in,
            "conv_w": conv_w, "w_proj_conv": w_proj_conv, "w_proj_attn": w_proj_attn,
            "att_bias": att_bias, "w_out": w_out,
            "norm2_w": norm2_w, "w_up": w_up, "w_down": w_down, "norm_f_w": norm_f_w}


def reference(x_prompt, x_sample, cache_k, cache_v, state_conv, page_table, norm1_w, w_in,
              conv_w, w_proj_conv, w_proj_attn, att_bias, w_out, norm2_w, w_up, w_down, norm_f_w):
    Bp, S = x_prompt.shape[0], x_prompt.shape[1]
    Bd = x_sample.shape[0]
    n_pages = page_table.shape[1]
    xp, xs = x_prompt, x_sample
    kp_l, vp_l, cp_l, ks_l, vs_l, cs_l = [], [], [], [], [], []
    for l in range(DEPTH):
        w = (norm1_w[l], w_in[l], conv_w[l], w_proj_conv[l], w_proj_attn[l], att_bias[l],
             w_out[l], norm2_w[l], w_up[l], w_down[l])
        conv0 = jnp.zeros((Bp, CONV_WIDTH - 1, D_CONV), xp.dtype)
        xp, cp, kp, vp = hybrid_layer(xp, conv0, None, None, *w)
        kp_l.append(kp.reshape(Bp, S // PAGE_SIZE, PAGE_SIZE, N_HEADS, HEAD_DIM))
        vp_l.append(vp.reshape(Bp, S // PAGE_SIZE, PAGE_SIZE, N_HEADS, HEAD_DIM))
        cp_l.append(cp)
        k_past = cache_k[l][page_table].reshape(Bd, n_pages * PAGE_SIZE, N_HEADS, HEAD_DIM)
        v_past = cache_v[l][page_table].reshape(Bd, n_pages * PAGE_SIZE, N_HEADS, HEAD_DIM)
        xs, cs, kn, vn = hybrid_layer(xs, state_conv[l], k_past, v_past, *w)
        ks_l.append(kn)
        vs_l.append(vn)
        cs_l.append(cs)
    y_prompt = rmsnorm(xp, norm_f_w)
    y_sample = rmsnorm(xs, norm_f_w)
    return (y_prompt, y_sample, jnp.stack(kp_l), jnp.stack(vp_l), jnp.stack(cp_l),
            jnp.stack(ks_l), jnp.stack(vs_l), jnp.stack(cs_l))
```

```python
import functools
import math

import jax
import jax.numpy as jnp
from jax import lax
from jax.experimental import pallas as pl
from jax.experimental.pallas import tpu as pltpu

D_MODEL = 1024
D_CONV = 512
N_HEADS = 8
HEAD_DIM = 64
D_ATT = N_HEADS * HEAD_DIM
D_FF = 4 * D_MODEL
D_IN = 3 * D_CONV + 3 * D_ATT + 2 * D_MODEL
PAGE_SIZE = 128
RMS_EPS = 1e-6
LOG2E = math.log2(math.e)

VMEM_LIMIT_BYTES = 56 * 1024 * 1024

TOKEN_TILE = 512
ATT_Q_TILE = 128
ATT_K_TILE = 256

F32 = jnp.float32
BF16 = jnp.bfloat16


def _dot(a, b):
    return jnp.dot(a, b, preferred_element_type=F32)


def _rmsnorm(x, w):
    return x * lax.rsqrt(jnp.mean(x * x, axis=-1, keepdims=True) + RMS_EPS) * w


def _softplus2(z):
    return jnp.maximum(z, 0.0) + jnp.log(1.0 + jnp.exp2(-jnp.abs(z))) * LOG2E


def _const_spec(shape):
    return pl.BlockSpec(shape, lambda *_: (0,) * len(shape))


def _inproj_body(x_ref, n1_ref, win_ref, convw_ref, wpc_ref, prev_fn,
                 q_ref, kb_ref, vb_ref, k_ref, v_ref, cu_ref, ma_ref, gb_ref):
    x = x_ref[...]
    hb = _rmsnorm(x, n1_ref[...]).astype(BF16)

    def proj(lo, hi):
        return _dot(hb, win_ref[:, lo:hi])

    o = 0
    b_g = proj(o, o + D_CONV); o += D_CONV
    c_g = proj(o, o + D_CONV); o += D_CONV
    u = proj(o, o + D_CONV); o += D_CONV
    cu = c_g * u
    cu_ref[...] = cu
    prev1, prev2 = prev_fn(cu)
    conv = convw_ref[0:1, :] * prev2 + convw_ref[1:2, :] * prev1 + convw_ref[2:3, :] * cu
    y_conv = _dot((b_g * conv).astype(BF16), wpc_ref[...])

    q = proj(o, o + D_ATT); o += D_ATT
    q_ref[...] = (q * (LOG2E * HEAD_DIM ** -0.5)).astype(BF16)
    k = proj(o, o + D_ATT); o += D_ATT
    k_ref[...] = k
    kb_ref[...] = k.astype(BF16)
    v = proj(o, o + D_ATT); o += D_ATT
    v_ref[...] = v
    vb_ref[...] = v.astype(BF16)
    gate_conv = proj(o, o + D_MODEL); o += D_MODEL
    ma_ref[...] = jax.nn.sigmoid(gate_conv) * y_conv
    gate_att = proj(o, o + D_MODEL); o += D_MODEL
    gb_ref[...] = jax.nn.sigmoid(gate_att)


def _inproj_prompt_kernel(tiles_per_seq, x_ref, n1_ref, win_ref, convw_ref, wpc_ref,
                          q_ref, kb_ref, vb_ref, k_ref, v_ref, cu_ref, ma_ref, gb_ref, carry_ref):
    tm = x_ref.shape[0]

    @pl.when(pl.program_id(0) % tiles_per_seq == 0)
    def _():
        carry_ref[...] = jnp.zeros_like(carry_ref)

    def prev_fn(cu):
        row = lax.broadcasted_iota(jnp.int32, cu.shape, 0)
        c1 = carry_ref[7:8, :]
        c2 = carry_ref[6:7, :]
        prev1 = jnp.where(row == 0, c1, pltpu.roll(cu, 1, 0))
        prev2 = jnp.where(row == 0, c2, jnp.where(row == 1, c1, pltpu.roll(cu, 2, 0)))
        carry_ref[...] = cu[tm - 8:, :]
        return prev1, prev2

    _inproj_body(x_ref, n1_ref, win_ref, convw_ref, wpc_ref, prev_fn,
                 q_ref, kb_ref, vb_ref, k_ref, v_ref, cu_ref, ma_ref, gb_ref)


def _inproj_sample_kernel(seq, x_ref, n1_ref, win_ref, convw_ref, wpc_ref, p1_ref, p2_ref,
                          q_ref, kb_ref, vb_ref, k_ref, v_ref, cu_ref, ma_ref, gb_ref):
    def prev_fn(cu):
        pos = lax.broadcasted_iota(jnp.int32, cu.shape, 0) % seq
        prev1 = jnp.where(pos == 0, p1_ref[...], pltpu.roll(cu, 1, 0))
        prev2 = jnp.where(pos < 2, p2_ref[...], pltpu.roll(cu, 2, 0))
        return prev1, prev2

    _inproj_body(x_ref, n1_ref, win_ref, convw_ref, wpc_ref, prev_fn,
                 q_ref, kb_ref, vb_ref, k_ref, v_ref, cu_ref, ma_ref, gb_ref)


def _inproj_out(n, tm):
    row = lambda d: pl.BlockSpec((tm, d), lambda i: (i, 0))
    shapes = [
        jax.ShapeDtypeStruct((n, D_ATT), BF16),
        jax.ShapeDtypeStruct((n, D_ATT), BF16),
        jax.ShapeDtypeStruct((n, D_ATT), BF16),
        jax.ShapeDtypeStruct((n, D_ATT), F32),
        jax.ShapeDtypeStruct((n, D_ATT), F32),
        jax.ShapeDtypeStruct((n, D_CONV), F32),
        jax.ShapeDtypeStruct((n, D_MODEL), F32),
        jax.ShapeDtypeStruct((n, D_MODEL), F32),
    ]
    specs = [row(D_ATT)] * 5 + [row(D_CONV), row(D_MODEL), row(D_MODEL)]
    return shapes, specs


def _inproj_weight_specs():
    return [_const_spec((1, D_MODEL)), _const_spec((D_MODEL, D_IN)),
            _const_spec((3, D_CONV)), _const_spec((D_CONV, D_MODEL))]


def _inproj_prompt(x, n1, win, convw, wpc, seq_len):
    n = x.shape[0]
    tm = TOKEN_TILE
    shapes, specs = _inproj_out(n, tm)
    return pl.pallas_call(
        functools.partial(_inproj_prompt_kernel, seq_len // tm),
        grid=(n // tm,),
        in_specs=[pl.BlockSpec((tm, D_MODEL), lambda i: (i, 0))] + _inproj_weight_specs(),
        out_specs=specs,
        out_shape=shapes,
        scratch_shapes=[pltpu.VMEM((8, D_CONV), F32)],
        compiler_params=pltpu.CompilerParams(
            dimension_semantics=("arbitrary",), vmem_limit_bytes=VMEM_LIMIT_BYTES),
        name="inproj_prompt",
    )(x, n1, win, convw, wpc)


def _inproj_sample(x, n1, win, convw, wpc, p1, p2, seq):
    n = x.shape[0]
    shapes, specs = _inproj_out(n, n)
    full = lambda d: pl.BlockSpec((n, d), lambda i: (0, 0))
    return pl.pallas_call(
        functools.partial(_inproj_sample_kernel, seq),
        grid=(1,),
        in_specs=[full(D_MODEL)] + _inproj_weight_specs() + [full(D_CONV), full(D_CONV)],
        out_specs=specs,
        out_shape=shapes,
        compiler_params=pltpu.CompilerParams(
            dimension_semantics=("arbitrary",), vmem_limit_bytes=VMEM_LIMIT_BYTES),
        name="inproj_sample",
    )(x, n1, win, convw, wpc, p1, p2)


def _prompt_attn_kernel(bias_ref, q_ref, k_ref, v_ref, u_ref, o_ref, acc_ref, r_ref):
    tq, tk = ATT_Q_TILE, ATT_K_TILE
    pair = pl.program_id(1)
    i = pl.program_id(2)

    qq = q_ref[0]
    lane = lax.broadcasted_iota(jnp.int32, qq.shape, 1)
    zero = jnp.zeros_like(qq)
    qs = jnp.concatenate([jnp.where(lane < HEAD_DIM, qq, zero),
                          jnp.where(lane >= HEAD_DIM, qq, zero)], axis=0)
    row = lax.broadcasted_iota(jnp.int32, (2 * tq, 1), 0)
    bias = jnp.where(row < tq, bias_ref[2 * pair], bias_ref[2 * pair + 1])

    acc_ref[...] = jnp.zeros_like(acc_ref)
    r_ref[...] = jnp.zeros_like(r_ref)
    u = u_ref[...]

    def block(j, mask):
        kk = k_ref[0, pl.ds(j * tk, tk), :]
        vv = v_ref[0, pl.ds(j * tk, tk), :]
        z = lax.dot_general(qs, kk, (((1,), (1,)), ((), ())), preferred_element_type=F32) + bias
        sp = _softplus2(z)
        if mask is not None:
            sp = jnp.where(mask, sp, 0.0)
        hi = sp.astype(BF16)
        lo = (sp - hi.astype(F32)).astype(BF16)
        cs = _dot(hi, u) + _dot(lo, u)
        w = jnp.exp2(z - cs - r_ref[...])
        if mask is not None:
            w = jnp.where(mask, w, 0.0)
        acc_ref[...] += _dot(w.astype(BF16), vv)
        r_ref[...] += cs[:, 0:1]

    jd = (i * tq) // tk
    qpos = i * tq + lax.broadcasted_iota(jnp.int32, (2 * tq, tk), 0) % tq
    kpos = jd * tk + lax.broadcasted_iota(jnp.int32, (2 * tq, tk), 1)
    block(jd, kpos < qpos)

    def body(n, carry):
        block(jd - 1 - n, None)
        return carry

    lax.fori_loop(0, jd, body, 0)

    acc = acc_ref[...]
    lane_o = lax.broadcasted_iota(jnp.int32, (tq, 2 * HEAD_DIM), 1)
    o_ref[0] = jnp.where(lane_o < HEAD_DIM, acc[:tq], acc[tq:]).astype(o_ref.dtype)


def _prompt_attn(q, k, v, bias2, u):
    b, s, _ = q.shape
    tq = ATT_Q_TILE
    kv_spec = pl.BlockSpec((1, s, 2 * HEAD_DIM), lambda bb, p, i: (bb, 0, p))
    return pl.pallas_call(
        _prompt_attn_kernel,
        grid=(b, N_HEADS // 2, s // tq),
        in_specs=[pl.BlockSpec(memory_space=pltpu.SMEM),
                  pl.BlockSpec((1, tq, 2 * HEAD_DIM), lambda bb, p, i: (bb, i, p)),
                  kv_spec, kv_spec,
                  _const_spec((ATT_K_TILE, ATT_K_TILE))],
        out_specs=pl.BlockSpec((1, tq, 2 * HEAD_DIM), lambda bb, p, i: (bb, i, p)),
        out_shape=jax.ShapeDtypeStruct((b, s, D_ATT), BF16),
        scratch_shapes=[pltpu.VMEM((2 * tq, 2 * HEAD_DIM), F32), pltpu.VMEM((2 * tq, 1), F32)],
        compiler_params=pltpu.CompilerParams(
            dimension_semantics=("arbitrary", "arbitrary", "arbitrary"),
            vmem_limit_bytes=VMEM_LIMIT_BYTES),
        name="prompt_attn",
    )(bias2, q, k, v, u)


def _sample_attn_kernel(pt_ref, qt_ref, kn_ref, vn_ref, kc_ref, vc_ref, bias_ref, o_ref,
                        acc_ref, r_ref):
    del pt_ref
    n = pl.program_id(1)
    hq = qt_ref.shape[2]
    nq = hq // N_HEADS
    qt = qt_ref[0]
    bias = bias_ref[...]
    sub = lax.broadcasted_iota(jnp.int32, (N_HEADS, hq), 0)
    lane = lax.broadcasted_iota(jnp.int32, (N_HEADS, hq), 1)
    same_head = sub == lane // nq

    def attend(kmat, vmat, npos, causal):
        z = (_dot(kmat.astype(BF16), qt) + bias).reshape(npos, N_HEADS, hq)
        sp = _softplus2(z)
        run = r_ref[...]
        ws = []
        for p in reversed(range(npos)):
            keep = same_head & (lane % nq > p) if causal else same_head
            run = run + (jnp.where(keep, sp[p], 0.0) if causal else sp[p])
            ws.append(jnp.where(keep, jnp.exp2(z[p] - run), 0.0))
        r_ref[...] = run
        w = jnp.stack(ws[::-1], axis=0).reshape(npos * N_HEADS, hq).astype(BF16)
        acc_ref[...] += lax.dot_general(w, vmat.astype(BF16), (((0,), (0,)), ((), ())),
                                        preferred_element_type=F32)

    @pl.when(n == 0)
    def _():
        acc_ref[...] = jnp.zeros_like(acc_ref)
        r_ref[...] = jnp.zeros_like(r_ref)
        attend(kn_ref[0], vn_ref[0], nq, True)

    @pl.when(n > 0)
    def _():
        page = kc_ref.shape[2]
        attend(kc_ref[0, 0].reshape(page * N_HEADS, HEAD_DIM),
               vc_ref[0, 0].reshape(page * N_HEADS, HEAD_DIM), page, False)

    @pl.when(n == pl.num_programs(1) - 1)
    def _():
        o_ref[0] = acc_ref[...]


def _sample_attn(page_table, qt, kn, vn, cache_k, cache_v, bias_row):
    bd, n_pages = page_table.shape
    hq = qt.shape[2]

    def page_map(b, n, pt):
        return (0, pt[b, n_pages - 1 - jnp.maximum(n - 1, 0)], 0, 0, 0)

    per_req = lambda shape: pl.BlockSpec((1,) + shape, lambda b, n, pt: (b, 0, 0))
    page_spec = pl.BlockSpec((1, 1, PAGE_SIZE, N_HEADS, HEAD_DIM), page_map)
    grid_spec = pltpu.PrefetchScalarGridSpec(
        num_scalar_prefetch=1,
        grid=(bd, n_pages + 1),
        in_specs=[per_req((HEAD_DIM, hq)), per_req((hq, HEAD_DIM)), per_req((hq, HEAD_DIM)),
                  page_spec, page_spec,
                  pl.BlockSpec((1, hq), lambda b, n, pt: (0, 0))],
        out_specs=per_req((hq, HEAD_DIM)),
        scratch_shapes=[pltpu.VMEM((hq, HEAD_DIM), F32), pltpu.VMEM((N_HEADS, hq), F32)],
    )
    return pl.pallas_call(
        _sample_attn_kernel,
        grid_spec=grid_spec,
        out_shape=jax.ShapeDtypeStruct((bd, hq, HEAD_DIM), F32),
        compiler_params=pltpu.CompilerParams(
            dimension_semantics=("arbitrary", "arbitrary"), vmem_limit_bytes=VMEM_LIMIT_BYTES),
        name="sample_attn",
    )(page_table, qt, kn, vn, cache_k, cache_v, bias_row)


def _output_kernel(x_ref, o_ref, ma_ref, gb_ref, wpa_ref, wout_ref, n2_ref, wup_ref, wdown_ref,
                   nf_ref, y_ref):
    y_att = _dot(o_ref[...], wpa_ref[...])
    merged = ma_ref[...] + gb_ref[...] * y_att
    x1 = x_ref[...] + _dot(merged.astype(BF16), wout_ref[...])
    h2 = _rmsnorm(x1, n2_ref[...]).astype(BF16)
    x2 = x1
    ff_chunk = D_MODEL
    for c in range(0, D_FF, ff_chunk):
        f = jnp.maximum(_dot(h2, wup_ref[:, c:c + ff_chunk]), 0.0)
        x2 = x2 + _dot((f * f).astype(BF16), wdown_ref[c:c + ff_chunk, :])
    y_ref[...] = _rmsnorm(x2, nf_ref[...])


def _output(x, o, ma, gb, wpa, wout, n2, wup, wdown, nf, tm):
    n = x.shape[0]
    row = lambda d: pl.BlockSpec((tm, d), lambda i: (i, 0))
    return pl.pallas_call(
        _output_kernel,
        grid=(n // tm,),
        in_specs=[row(D_MODEL), row(D_ATT), row(D_MODEL), row(D_MODEL),
                  _const_spec((D_ATT, D_MODEL)), _const_spec((D_MODEL, D_MODEL)),
                  _const_spec((1, D_MODEL)), _const_spec((D_MODEL, D_FF)),
                  _const_spec((D_FF, D_MODEL)), _const_spec((1, D_MODEL))],
        out_specs=row(D_MODEL),
        out_shape=jax.ShapeDtypeStruct((n, D_MODEL), F32),
        compiler_params=pltpu.CompilerParams(
            dimension_semantics=("arbitrary",), vmem_limit_bytes=VMEM_LIMIT_BYTES),
        name="output_mlp",
    )(x, o, ma, gb, wpa, wout, n2, wup, wdown, nf)


def kernel(x_prompt, x_sample, cache_k, cache_v, state_conv, page_table, norm1_w, w_in, conv_w,
           w_proj_conv, w_proj_attn, att_bias, w_out, norm2_w, w_up, w_down, norm_f_w):
    depth = w_in.shape[0]
    assert depth == 1, "single-layer stack"
    bp, s, _ = x_prompt.shape
    bd, ls, _ = x_sample.shape
    n_pages = page_table.shape[1]

    n1 = norm1_w[0][None, :]
    n2 = norm2_w[0][None, :]
    nf = norm_f_w[None, :]
    win = w_in[0].astype(BF16)
    wpc = w_proj_conv[0].astype(BF16)
    wpa = w_proj_attn[0].astype(BF16)
    wout = w_out[0].astype(BF16)
    wup = w_up[0].astype(BF16)
    wdown = w_down[0].astype(BF16)
    convw = conv_w[0]
    bias2 = att_bias[0] * LOG2E

    xp = x_prompt.reshape(bp * s, D_MODEL)
    qp, kbp, vbp, kp, vp, cup, map_, gbp = _inproj_prompt(xp, n1, win, convw, wpc, s)
    tri = (lax.broadcasted_iota(jnp.int32, (ATT_K_TILE, ATT_K_TILE), 0)
           >= lax.broadcasted_iota(jnp.int32, (ATT_K_TILE, ATT_K_TILE), 1)).astype(BF16)
    op = _prompt_attn(qp.reshape(bp, s, D_ATT), kbp.reshape(bp, s, D_ATT),
                      vbp.reshape(bp, s, D_ATT), bias2, tri)
    yp = _output(xp, op.reshape(bp * s, D_ATT), map_, gbp, wpa, wout, n2, wup, wdown, nf,
                 TOKEN_TILE)

    xs = x_sample.reshape(bd * ls, D_MODEL)
    st = state_conv[0]
    zeros = jnp.zeros((bd, ls, D_CONV), F32)
    p1 = zeros.at[:, 0].set(st[:, 1]).reshape(bd * ls, D_CONV)
    p2 = zeros.at[:, 0].set(st[:, 0]).at[:, 1].set(st[:, 1]).reshape(bd * ls, D_CONV)
    qs, _, _, ks, vs, cus, mas, gbs = _inproj_sample(xs, n1, win, convw, wpc, p1, p2, ls)
    hq = N_HEADS * ls
    qt = qs.reshape(bd, ls, N_HEADS, HEAD_DIM).transpose(0, 3, 2, 1).reshape(bd, HEAD_DIM, hq)
    bias_row = jnp.repeat(bias2, ls)[None, :]
    os_ = _sample_attn(page_table, qt, ks.reshape(bd, hq, HEAD_DIM), vs.reshape(bd, hq, HEAD_DIM),
                       cache_k, cache_v, bias_row)
    os_ = os_.reshape(bd, N_HEADS, ls, HEAD_DIM).transpose(0, 2, 1, 3).reshape(bd * ls, D_ATT)
    ys = _output(xs, os_.astype(BF16), mas, gbs, wpa, wout, n2, wup, wdown, nf, bd * ls)

    return (
        yp.reshape(bp, s, D_MODEL),
        ys.reshape(bd, ls, D_MODEL),
        kp.reshape(depth, bp, s // PAGE_SIZE, PAGE_SIZE, N_HEADS, HEAD_DIM),
        vp.reshape(depth, bp, s // PAGE_SIZE, PAGE_SIZE, N_HEADS, HEAD_DIM),
        cup.reshape(bp, s, D_CONV)[None, :, s - 2:, :],
        ks.reshape(depth, bd, ls, N_HEADS, HEAD_DIM),
        vs.reshape(depth, bd, ls, N_HEADS, HEAD_DIM),
        cus.reshape(bd, ls, D_CONV)[None, :, ls - 2:, :],
    )
```

```python
import functools
import math

import jax
import jax.numpy as jnp
from jax import lax
from jax.experimental import pallas as pl
from jax.experimental.pallas import tpu as pltpu

D_MODEL = 1024
D_CONV = 512
N_HEADS = 8
HEAD_DIM = 64
D_ATT = N_HEADS * HEAD_DIM
D_FF = 4 * D_MODEL
D_IN = 3 * D_CONV + 3 * D_ATT + 2 * D_MODEL
PAGE_SIZE = 128
RMS_EPS = 1e-6
LOG2E = math.log2(math.e)

VMEM_LIMIT_BYTES = 56 * 1024 * 1024

TOKEN_TILE = 512
ATT_Q_TILE = 256
ATT_K_TILE = 256
SAMPLE_PAGES_PER_STEP = 8

F32 = jnp.float32
BF16 = jnp.bfloat16


def _dot(a, b):
    return jnp.dot(a, b, preferred_element_type=F32)


def _dot_nt(a, b):
    return lax.dot_general(a, b, (((1,), (1,)), ((), ())), preferred_element_type=F32)


def _rmsnorm(x, w):
    return x * lax.rsqrt(jnp.mean(x * x, axis=-1, keepdims=True) + RMS_EPS) * w


def _softplus2(z):
    neg_abs = lax.bitcast_convert_type(
        lax.bitcast_convert_type(z, jnp.uint32) | jnp.uint32(0x80000000), F32)
    return jnp.maximum(z, 0.0) + jnp.log(1.0 + jnp.exp2(neg_abs)) * LOG2E


def _suffix_sums(sp, tri2):
    hi = sp.astype(BF16)
    lo = (sp - hi.astype(F32)).astype(BF16)
    return _dot(jnp.concatenate([hi, lo], axis=1), tri2)


def _tri2(n):
    j = lax.broadcasted_iota(jnp.int32, (2 * n, n), 0) % n
    s = lax.broadcasted_iota(jnp.int32, (2 * n, n), 1)
    return (j >= s).astype(BF16)


def _const_spec(shape):
    return pl.BlockSpec(shape, lambda *_: (0,) * len(shape))


def _inproj_body(x_ref, n1_ref, win_ref, convw_ref, wpc_ref, prev_fn,
                 q_ref, kb_ref, vb_ref, k_ref, v_ref, cu_ref, ma_ref, gb_ref):
    x = x_ref[...]
    hb = _rmsnorm(x, n1_ref[...]).astype(BF16)

    def proj(lo, hi):
        return _dot(hb, win_ref[:, lo:hi])

    o = 0
    b_g = proj(o, o + D_CONV); o += D_CONV
    c_g = proj(o, o + D_CONV); o += D_CONV
    u = proj(o, o + D_CONV); o += D_CONV
    cu = c_g * u
    cu_ref[...] = cu
    prev1, prev2 = prev_fn(cu)
    conv = convw_ref[0:1, :] * prev2 + convw_ref[1:2, :] * prev1 + convw_ref[2:3, :] * cu
    y_conv = _dot((b_g * conv).astype(BF16), wpc_ref[...])

    q = proj(o, o + D_ATT); o += D_ATT
    q_ref[...] = (q * (LOG2E * HEAD_DIM ** -0.5)).astype(BF16)
    k = proj(o, o + D_ATT); o += D_ATT
    k_ref[...] = k
    kb_ref[...] = k.astype(BF16)
    v = proj(o, o + D_ATT); o += D_ATT
    v_ref[...] = v
    vb_ref[...] = v.astype(BF16)
    gate_conv = proj(o, o + D_MODEL); o += D_MODEL
    ma_ref[...] = jax.nn.sigmoid(gate_conv) * y_conv
    gate_att = proj(o, o + D_MODEL); o += D_MODEL
    gb_ref[...] = jax.nn.sigmoid(gate_att)


def _inproj_prompt_kernel(tiles_per_seq, x_ref, n1_ref, win_ref, convw_ref, wpc_ref,
                          q_ref, kb_ref, vb_ref, k_ref, v_ref, cu_ref, ma_ref, gb_ref, carry_ref):
    tm = x_ref.shape[0]

    @pl.when(pl.program_id(0) % tiles_per_seq == 0)
    def _():
        carry_ref[...] = jnp.zeros_like(carry_ref)

    def prev_fn(cu):
        row = lax.broadcasted_iota(jnp.int32, cu.shape, 0)
        c1 = carry_ref[7:8, :]
        c2 = carry_ref[6:7, :]
        prev1 = jnp.where(row == 0, c1, pltpu.roll(cu, 1, 0))
        prev2 = jnp.where(row == 0, c2, jnp.where(row == 1, c1, pltpu.roll(cu, 2, 0)))
        carry_ref[...] = cu[tm - 8:, :]
        return prev1, prev2

    _inproj_body(x_ref, n1_ref, win_ref, convw_ref, wpc_ref, prev_fn,
                 q_ref, kb_ref, vb_ref, k_ref, v_ref, cu_ref, ma_ref, gb_ref)


def _inproj_sample_kernel(seq, x_ref, n1_ref, win_ref, convw_ref, wpc_ref, p1_ref, p2_ref,
                          q_ref, kb_ref, vb_ref, k_ref, v_ref, cu_ref, ma_ref, gb_ref):
    def prev_fn(cu):
        pos = lax.broadcasted_iota(jnp.int32, cu.shape, 0) % seq
        prev1 = jnp.where(pos == 0, p1_ref[...], pltpu.roll(cu, 1, 0))
        prev2 = jnp.where(pos < 2, p2_ref[...], pltpu.roll(cu, 2, 0))
        return prev1, prev2

    _inproj_body(x_ref, n1_ref, win_ref, convw_ref, wpc_ref, prev_fn,
                 q_ref, kb_ref, vb_ref, k_ref, v_ref, cu_ref, ma_ref, gb_ref)


def _inproj_out(n, tm):
    row = lambda d: pl.BlockSpec((tm, d), lambda i: (i, 0))
    shapes = [
        jax.ShapeDtypeStruct((n, D_ATT), BF16),
        jax.ShapeDtypeStruct((n, D_ATT), BF16),
        jax.ShapeDtypeStruct((n, D_ATT), BF16),
        jax.ShapeDtypeStruct((n, D_ATT), F32),
        jax.ShapeDtypeStruct((n, D_ATT), F32),
        jax.ShapeDtypeStruct((n, D_CONV), F32),
        jax.ShapeDtypeStruct((n, D_MODEL), F32),
        jax.ShapeDtypeStruct((n, D_MODEL), F32),
    ]
    specs = [row(D_ATT)] * 5 + [row(D_CONV), row(D_MODEL), row(D_MODEL)]
    return shapes, specs


def _inproj_weight_specs():
    return [_const_spec((1, D_MODEL)), _const_spec((D_MODEL, D_IN)),
            _const_spec((3, D_CONV)), _const_spec((D_CONV, D_MODEL))]


def _inproj_prompt(x, n1, win, convw, wpc, seq_len):
    n = x.shape[0]
    tm = TOKEN_TILE
    shapes, specs = _inproj_out(n, tm)
    return pl.pallas_call(
        functools.partial(_inproj_prompt_kernel, seq_len // tm),
        grid=(n // tm,),
        in_specs=[pl.BlockSpec((tm, D_MODEL), lambda i: (i, 0))] + _inproj_weight_specs(),
        out_specs=specs,
        out_shape=shapes,
        scratch_shapes=[pltpu.VMEM((8, D_CONV), F32)],
        compiler_params=pltpu.CompilerParams(
            dimension_semantics=("arbitrary",), vmem_limit_bytes=VMEM_LIMIT_BYTES),
        name="inproj_prompt",
    )(x, n1, win, convw, wpc)


def _inproj_sample(x, n1, win, convw, wpc, p1, p2, seq):
    n = x.shape[0]
    shapes, specs = _inproj_out(n, n)
    full = lambda d: pl.BlockSpec((n, d), lambda i: (0, 0))
    return pl.pallas_call(
        functools.partial(_inproj_sample_kernel, seq),
        grid=(1,),
        in_specs=[full(D_MODEL)] + _inproj_weight_specs() + [full(D_CONV), full(D_CONV)],
        out_specs=specs,
        out_shape=shapes,
        compiler_params=pltpu.CompilerParams(
            dimension_semantics=("arbitrary",), vmem_limit_bytes=VMEM_LIMIT_BYTES),
        name="inproj_sample",
    )(x, n1, win, convw, wpc, p1, p2)


def _stick_breaking_step(logits_fns, value_fns, tri2, mask, acc_ref, r_ref):
    zs = [f() for f in logits_fns]
    sums = []
    for z in zs:
        sp = _softplus2(z)
        if mask is not None:
            sp = jnp.where(mask, sp, 0.0)
        sums.append(_suffix_sums(sp, tri2))
    r = r_ref[...]
    acc = acc_ref[...]
    for z, cs, value_fn in zip(zs, sums, value_fns):
        w = jnp.exp2(z - cs - r)
        if mask is not None:
            w = jnp.where(mask, w, 0.0)
        acc = acc + value_fn(w.astype(BF16))
        r = r + cs[:, 0:1]
    acc_ref[...] = acc
    r_ref[...] = r


def _prompt_attn_kernel(bias_ref, q_ref, k_ref, v_ref, tri_ref, o_ref, acc_ref, r_ref):
    tq, tk = ATT_Q_TILE, ATT_K_TILE
    pair = pl.program_id(1)
    i = pl.program_id(2)

    qq = q_ref[0]
    lane = lax.broadcasted_iota(jnp.int32, qq.shape, 1)
    zero = jnp.zeros_like(qq)
    qs = jnp.concatenate([jnp.where(lane < HEAD_DIM, qq, zero),
                          jnp.where(lane >= HEAD_DIM, qq, zero)], axis=0)
    row = lax.broadcasted_iota(jnp.int32, (2 * tq, 1), 0)
    bias = jnp.where(row < tq, bias_ref[2 * pair], bias_ref[2 * pair + 1])

    acc_ref[...] = jnp.zeros_like(acc_ref)
    r_ref[...] = jnp.zeros_like(r_ref)
    tri2 = tri_ref[...]

    def step(js, mask):
        def logits_fn(j):
            start = pl.multiple_of(j * tk, tk)
            return lambda: _dot_nt(qs, k_ref[0, pl.ds(start, tk), :]) + bias

        def value_fn(j):
            start = pl.multiple_of(j * tk, tk)
            return lambda w: _dot(w, v_ref[0, pl.ds(start, tk), :])

        _stick_breaking_step([logits_fn(j) for j in js], [value_fn(j) for j in js],
                             tri2, mask, acc_ref, r_ref)

    qpos = lax.broadcasted_iota(jnp.int32, (2 * tq, tk), 0) % tq
    kpos = lax.broadcasted_iota(jnp.int32, (2 * tq, tk), 1)
    step([i], kpos < qpos)

    odd = i % 2

    @pl.when(odd == 1)
    def _():
        step([i - 1], None)

    def body(n, carry):
        top = i - 1 - odd - 2 * n
        step([top, top - 1], None)
        return carry

    lax.fori_loop(0, i // 2, body, 0)

    acc = acc_ref[...]
    lane_o = lax.broadcasted_iota(jnp.int32, (tq, 2 * HEAD_DIM), 1)
    o_ref[0] = jnp.where(lane_o < HEAD_DIM, acc[:tq], acc[tq:]).astype(o_ref.dtype)


def _prompt_attn(q, k, v, bias2):
    b, s, _ = q.shape
    tq, tk = ATT_Q_TILE, ATT_K_TILE
    assert tq == tk and s % tq == 0
    kv_spec = pl.BlockSpec((1, s, 2 * HEAD_DIM), lambda bb, p, i: (bb, 0, p))
    return pl.pallas_call(
        _prompt_attn_kernel,
        grid=(b, N_HEADS // 2, s // tq),
        in_specs=[pl.BlockSpec(memory_space=pltpu.SMEM),
                  pl.BlockSpec((1, tq, 2 * HEAD_DIM), lambda bb, p, i: (bb, i, p)),
                  kv_spec, kv_spec,
                  _const_spec((2 * tk, tk))],
        out_specs=pl.BlockSpec((1, tq, 2 * HEAD_DIM), lambda bb, p, i: (bb, i, p)),
        out_shape=jax.ShapeDtypeStruct((b, s, D_ATT), BF16),
        scratch_shapes=[pltpu.VMEM((2 * tq, 2 * HEAD_DIM), F32), pltpu.VMEM((2 * tq, 1), F32)],
        compiler_params=pltpu.CompilerParams(
            dimension_semantics=("arbitrary", "arbitrary", "arbitrary"),
            vmem_limit_bytes=VMEM_LIMIT_BYTES),
        name="prompt_attn",
    )(bias2, q, k, v, _tri2(tk))


def _sample_attn_kernel(pt_ref, qbd_ref, knt_ref, vnt_ref, tri_ref, bias_ref, *refs):
    del pt_ref
    npg = SAMPLE_PAGES_PER_STEP
    k_refs, v_refs = refs[:npg], refs[npg:2 * npg]
    o_ref, acc_ref, r_ref = refs[2 * npg:]
    n = pl.program_id(1)
    hq = qbd_ref.shape[1]
    nq = hq // N_HEADS
    qbd = qbd_ref[0]
    bias = bias_ref[...]
    tri2 = tri_ref[...]

    def step(pages, mask):
        logits_fns = [lambda kt=kt: _dot(qbd, kt().astype(BF16)) + bias for kt, _ in pages]
        value_fns = [lambda w, vt=vt: _dot_nt(w, vt().astype(BF16)) for _, vt in pages]
        _stick_breaking_step(logits_fns, value_fns, tri2, mask, acc_ref, r_ref)

    @pl.when(n == 0)
    def _():
        acc_ref[...] = jnp.zeros_like(acc_ref)
        r_ref[...] = jnp.zeros_like(r_ref)
        kpos = lax.broadcasted_iota(jnp.int32, (hq, PAGE_SIZE), 1)
        qpos = lax.broadcasted_iota(jnp.int32, (hq, PAGE_SIZE), 0) % nq
        step([(lambda: knt_ref[0], lambda: vnt_ref[0])], kpos < qpos)

    def page(ref):
        return lambda: ref[0, 0].reshape(N_HEADS * HEAD_DIM, PAGE_SIZE)

    step([(page(k), page(v)) for k, v in zip(k_refs, v_refs)], None)

    @pl.when(n == pl.num_programs(1) - 1)
    def _():
        acc = acc_ref[...]
        for h in range(N_HEADS):
            o_ref[0, h * nq:(h + 1) * nq, :] = acc[h * nq:(h + 1) * nq,
                                                   h * HEAD_DIM:(h + 1) * HEAD_DIM]


def _sample_attn(page_table, qbd, knt, vnt, cache_kt, cache_vt, bias_col):
    bd, n_pages = page_table.shape
    hq = qbd.shape[1]
    npg = SAMPLE_PAGES_PER_STEP
    assert n_pages % npg == 0

    def page_spec(p):
        return pl.BlockSpec(
            (1, 1, N_HEADS, HEAD_DIM, PAGE_SIZE),
            lambda b, n, pt: (0, pt[b, n_pages - 1 - (n * npg + p)], 0, 0, 0))

    per_req = lambda shape: pl.BlockSpec((1,) + shape, lambda b, n, pt: (b, 0, 0))
    const = lambda shape: pl.BlockSpec(shape, lambda b, n, pt: (0, 0))
    pages = [page_spec(p) for p in range(npg)]
    grid_spec = pltpu.PrefetchScalarGridSpec(
        num_scalar_prefetch=1,
        grid=(bd, n_pages // npg),
        in_specs=[per_req((hq, D_ATT)), per_req((D_ATT, PAGE_SIZE)), per_req((D_ATT, PAGE_SIZE)),
                  const((2 * PAGE_SIZE, PAGE_SIZE)), const((hq, 1))] + pages + pages,
        out_specs=per_req((hq, HEAD_DIM)),
        scratch_shapes=[pltpu.VMEM((hq, D_ATT), F32), pltpu.VMEM((hq, 1), F32)],
    )
    return pl.pallas_call(
        _sample_attn_kernel,
        grid_spec=grid_spec,
        out_shape=jax.ShapeDtypeStruct((bd, hq, HEAD_DIM), F32),
        compiler_params=pltpu.CompilerParams(
            dimension_semantics=("arbitrary", "arbitrary"), vmem_limit_bytes=VMEM_LIMIT_BYTES),
        name="sample_attn",
    )(page_table, qbd, knt, vnt, _tri2(PAGE_SIZE), bias_col,
      *([cache_kt] * npg), *([cache_vt] * npg))


def _output_kernel(x_ref, o_ref, ma_ref, gb_ref, wpa_ref, wout_ref, n2_ref, wup_ref, wdown_ref,
                   nf_ref, y_ref):
    y_att = _dot(o_ref[...], wpa_ref[...])
    merged = ma_ref[...] + gb_ref[...] * y_att
    x1 = x_ref[...] + _dot(merged.astype(BF16), wout_ref[...])
    h2 = _rmsnorm(x1, n2_ref[...]).astype(BF16)
    x2 = x1
    ff_chunk = D_MODEL
    for c in range(0, D_FF, ff_chunk):
        f = jnp.maximum(_dot(h2, wup_ref[:, c:c + ff_chunk]), 0.0)
        x2 = x2 + _dot((f * f).astype(BF16), wdown_ref[c:c + ff_chunk, :])
    y_ref[...] = _rmsnorm(x2, nf_ref[...])


def _output(x, o, ma, gb, wpa, wout, n2, wup, wdown, nf, tm):
    n = x.shape[0]
    row = lambda d: pl.BlockSpec((tm, d), lambda i: (i, 0))
    return pl.pallas_call(
        _output_kernel,
        grid=(n // tm,),
        in_specs=[row(D_MODEL), row(D_ATT), row(D_MODEL), row(D_MODEL),
                  _const_spec((D_ATT, D_MODEL)), _const_spec((D_MODEL, D_MODEL)),
                  _const_spec((1, D_MODEL)), _const_spec((D_MODEL, D_FF)),
                  _const_spec((D_FF, D_MODEL)), _const_spec((1, D_MODEL))],
        out_specs=row(D_MODEL),
        out_shape=jax.ShapeDtypeStruct((n, D_MODEL), F32),
        compiler_params=pltpu.CompilerParams(
            dimension_semantics=("arbitrary",), vmem_limit_bytes=VMEM_LIMIT_BYTES),
        name="output_mlp",
    )(x, o, ma, gb, wpa, wout, n2, wup, wdown, nf)


def kernel(x_prompt, x_sample, cache_k, cache_v, state_conv, page_table, norm1_w, w_in, conv_w,
           w_proj_conv, w_proj_attn, att_bias, w_out, norm2_w, w_up, w_down, norm_f_w):
    depth = w_in.shape[0]
    assert depth == 1, "single-layer stack"
    bp, s, _ = x_prompt.shape
    bd, ls, _ = x_sample.shape

    n1 = norm1_w[0][None, :]
    n2 = norm2_w[0][None, :]
    nf = norm_f_w[None, :]
    win = w_in[0].astype(BF16)
    wpc = w_proj_conv[0].astype(BF16)
    wpa = w_proj_attn[0].astype(BF16)
    wout = w_out[0].astype(BF16)
    wup = w_up[0].astype(BF16)
    wdown = w_down[0].astype(BF16)
    convw = conv_w[0]
    bias2 = att_bias[0] * LOG2E

    xp = x_prompt.reshape(bp * s, D_MODEL)
    qp, kbp, vbp, kp, vp, cup, map_, gbp = _inproj_prompt(xp, n1, win, convw, wpc, s)
    op = _prompt_attn(qp.reshape(bp, s, D_ATT), kbp.reshape(bp, s, D_ATT),
                      vbp.reshape(bp, s, D_ATT), bias2)
    yp = _output(xp, op.reshape(bp * s, D_ATT), map_, gbp, wpa, wout, n2, wup, wdown, nf,
                 TOKEN_TILE)

    xs = x_sample.reshape(bd * ls, D_MODEL)
    st = state_conv[0]
    zeros = jnp.zeros((bd, ls, D_CONV), F32)
    p1 = zeros.at[:, 0].set(st[:, 1]).reshape(bd * ls, D_CONV)
    p2 = zeros.at[:, 0].set(st[:, 0]).at[:, 1].set(st[:, 1]).reshape(bd * ls, D_CONV)
    qs, _, _, ks, vs, cus, mas, gbs = _inproj_sample(xs, n1, win, convw, wpc, p1, p2, ls)
    hq = N_HEADS * ls
    q4 = qs.reshape(bd, ls, N_HEADS, HEAD_DIM).transpose(0, 2, 1, 3)
    eye = jnp.eye(N_HEADS, dtype=BF16)
    qbd = (q4[:, :, :, None, :] * eye[None, :, None, :, None]).reshape(bd, hq, D_ATT)
    bias_col = jnp.repeat(bias2, ls)[:, None]
    pad = ((0, 0), (0, 0), (0, PAGE_SIZE - ls))
    knt = jnp.pad(ks.reshape(bd, ls, D_ATT).transpose(0, 2, 1), pad)
    vnt = jnp.pad(vs.reshape(bd, ls, D_ATT).transpose(0, 2, 1), pad)
    cache_kt = cache_k.transpose(0, 1, 3, 4, 2)
    cache_vt = cache_v.transpose(0, 1, 3, 4, 2)
    os_ = _sample_attn(page_table, qbd, knt, vnt, cache_kt, cache_vt, bias_col)
    os_ = os_.reshape(bd, N_HEADS, ls, HEAD_DIM).transpose(0, 2, 1, 3).reshape(bd * ls, D_ATT)
    ys = _output(xs, os_.astype(BF16), mas, gbs, wpa, wout, n2, wup, wdown, nf, bd * ls)

    return (
        yp.reshape(bp, s, D_MODEL),
        ys.reshape(bd, ls, D_MODEL),
        kp.reshape(depth, bp, s // PAGE_SIZE, PAGE_SIZE, N_HEADS, HEAD_DIM),
        vp.reshape(depth, bp, s // PAGE_SIZE, PAGE_SIZE, N_HEADS, HEAD_DIM),
        cup.reshape(bp, s, D_CONV)[None, :, s - 2:, :],
        ks.reshape(depth, bd, ls, N_HEADS, HEAD_DIM),
        vs.reshape(depth, bd, ls, N_HEADS, HEAD_DIM),
        cus.reshape(bd, ls, D_CONV)[None, :, ls - 2:, :],
    )
```

```python
import functools
import math

import jax
import jax.numpy as jnp
from jax import lax
from jax.experimental import pallas as pl
from jax.experimental.pallas import tpu as pltpu

D_MODEL = 1024
D_CONV = 512
N_HEADS = 8
HEAD_DIM = 64
D_ATT = N_HEADS * HEAD_DIM
D_FF = 4 * D_MODEL
D_IN = 3 * D_CONV + 3 * D_ATT + 2 * D_MODEL
PAGE_SIZE = 128
LANES = 128
RMS_EPS = 1e-6
LOG2E = math.log2(math.e)

VMEM_LIMIT_BYTES = 56 * 1024 * 1024

TOKEN_TILE = 512
ATT_Q_TILE = 256
ATT_K_TILE = 256
SAMPLE_PAGES_PER_STEP = 8

F32 = jnp.float32
BF16 = jnp.bfloat16


def _dot(a, b):
    return jnp.dot(a, b, preferred_element_type=F32)


def _dot_nt(a, b):
    return lax.dot_general(a, b, (((1,), (1,)), ((), ())), preferred_element_type=F32)


def _rmsnorm(x, w):
    return x * lax.rsqrt(jnp.mean(x * x, axis=-1, keepdims=True) + RMS_EPS) * w


def _softplus2(z):
    neg_abs = lax.bitcast_convert_type(
        lax.bitcast_convert_type(z, jnp.uint32) | jnp.uint32(0x80000000), F32)
    return jnp.maximum(z, 0.0) + jnp.log(1.0 + jnp.exp2(neg_abs)) * LOG2E


def _softplus2_operand(z, mask):
    sp = _softplus2(z)
    if mask is not None:
        sp = jnp.where(mask, sp, 0.0)
    return sp.astype(BF16)


def _weights(z, cs, r, mask):
    r_full = jnp.concatenate([r] * (z.shape[1] // LANES), axis=1) if z.shape[1] > LANES else r
    w = jnp.exp2(z - cs - r_full)
    if mask is not None:
        w = jnp.where(mask, w, 0.0)
    return w.astype(BF16)


def _row_total(cs):
    return jnp.broadcast_to(cs[:, 0:1], (cs.shape[0], LANES))


def _tri(n):
    j = lax.broadcasted_iota(jnp.int32, (n, n), 0)
    s = lax.broadcasted_iota(jnp.int32, (n, n), 1)
    return (j >= s).astype(BF16)


def _const_spec(shape):
    return pl.BlockSpec(shape, lambda *_: (0,) * len(shape))


def _inproj_body(x_ref, n1_ref, win_ref, convw_ref, wpc_ref, prev_fn,
                 q_ref, kb_ref, vb_ref, k_ref, v_ref, cu_ref, ma_ref, gb_ref):
    x = x_ref[...]
    hb = _rmsnorm(x, n1_ref[...]).astype(BF16)

    def proj(lo, hi):
        return _dot(hb, win_ref[:, lo:hi])

    o = 0
    b_g = proj(o, o + D_CONV); o += D_CONV
    c_g = proj(o, o + D_CONV); o += D_CONV
    u = proj(o, o + D_CONV); o += D_CONV
    cu = c_g * u
    cu_ref[...] = cu
    prev1, prev2 = prev_fn(cu)
    conv = convw_ref[0:1, :] * prev2 + convw_ref[1:2, :] * prev1 + convw_ref[2:3, :] * cu
    y_conv = _dot((b_g * conv).astype(BF16), wpc_ref[...])

    q = proj(o, o + D_ATT); o += D_ATT
    q_ref[...] = (q * (LOG2E * HEAD_DIM ** -0.5)).astype(BF16)
    k = proj(o, o + D_ATT); o += D_ATT
    k_ref[...] = k
    kb_ref[...] = k.astype(BF16)
    v = proj(o, o + D_ATT); o += D_ATT
    v_ref[...] = v
    vb_ref[...] = v.astype(BF16)
    gate_conv = proj(o, o + D_MODEL); o += D_MODEL
    ma_ref[...] = jax.nn.sigmoid(gate_conv) * y_conv
    gate_att = proj(o, o + D_MODEL); o += D_MODEL
    gb_ref[...] = jax.nn.sigmoid(gate_att)


def _inproj_prompt_kernel(tiles_per_seq, x_ref, n1_ref, win_ref, convw_ref, wpc_ref,
                          q_ref, kb_ref, vb_ref, k_ref, v_ref, cu_ref, ma_ref, gb_ref, carry_ref):
    tm = x_ref.shape[0]

    @pl.when(pl.program_id(0) % tiles_per_seq == 0)
    def _():
        carry_ref[...] = jnp.zeros_like(carry_ref)

    def prev_fn(cu):
        row = lax.broadcasted_iota(jnp.int32, cu.shape, 0)
        c1 = carry_ref[7:8, :]
        c2 = carry_ref[6:7, :]
        prev1 = jnp.where(row == 0, c1, pltpu.roll(cu, 1, 0))
        prev2 = jnp.where(row == 0, c2, jnp.where(row == 1, c1, pltpu.roll(cu, 2, 0)))
        carry_ref[...] = cu[tm - 8:, :]
        return prev1, prev2

    _inproj_body(x_ref, n1_ref, win_ref, convw_ref, wpc_ref, prev_fn,
                 q_ref, kb_ref, vb_ref, k_ref, v_ref, cu_ref, ma_ref, gb_ref)


def _inproj_sample_kernel(seq, x_ref, n1_ref, win_ref, convw_ref, wpc_ref, p1_ref, p2_ref,
                          q_ref, kb_ref, vb_ref, k_ref, v_ref, cu_ref, ma_ref, gb_ref):
    def prev_fn(cu):
        pos = lax.broadcasted_iota(jnp.int32, cu.shape, 0) % seq
        prev1 = jnp.where(pos == 0, p1_ref[...], pltpu.roll(cu, 1, 0))
        prev2 = jnp.where(pos < 2, p2_ref[...], pltpu.roll(cu, 2, 0))
        return prev1, prev2

    _inproj_body(x_ref, n1_ref, win_ref, convw_ref, wpc_ref, prev_fn,
                 q_ref, kb_ref, vb_ref, k_ref, v_ref, cu_ref, ma_ref, gb_ref)


def _inproj_out(n, tm):
    row = lambda d: pl.BlockSpec((tm, d), lambda i: (i, 0))
    shapes = [
        jax.ShapeDtypeStruct((n, D_ATT), BF16),
        jax.ShapeDtypeStruct((n, D_ATT), BF16),
        jax.ShapeDtypeStruct((n, D_ATT), BF16),
        jax.ShapeDtypeStruct((n, D_ATT), F32),
        jax.ShapeDtypeStruct((n, D_ATT), F32),
        jax.ShapeDtypeStruct((n, D_CONV), F32),
        jax.ShapeDtypeStruct((n, D_MODEL), F32),
        jax.ShapeDtypeStruct((n, D_MODEL), F32),
    ]
    specs = [row(D_ATT)] * 5 + [row(D_CONV), row(D_MODEL), row(D_MODEL)]
    return shapes, specs


def _inproj_weight_specs():
    return [_const_spec((1, D_MODEL)), _const_spec((D_MODEL, D_IN)),
            _const_spec((3, D_CONV)), _const_spec((D_CONV, D_MODEL))]


def _inproj_prompt(x, n1, win, convw, wpc, seq_len):
    n = x.shape[0]
    tm = TOKEN_TILE
    shapes, specs = _inproj_out(n, tm)
    return pl.pallas_call(
        functools.partial(_inproj_prompt_kernel, seq_len // tm),
        grid=(n // tm,),
        in_specs=[pl.BlockSpec((tm, D_MODEL), lambda i: (i, 0))] + _inproj_weight_specs(),
        out_specs=specs,
        out_shape=shapes,
        scratch_shapes=[pltpu.VMEM((8, D_CONV), F32)],
        compiler_params=pltpu.CompilerParams(
            dimension_semantics=("arbitrary",), vmem_limit_bytes=VMEM_LIMIT_BYTES),
        name="inproj_prompt",
    )(x, n1, win, convw, wpc)


def _inproj_sample(x, n1, win, convw, wpc, p1, p2, seq):
    n = x.shape[0]
    shapes, specs = _inproj_out(n, n)
    full = lambda d: pl.BlockSpec((n, d), lambda i: (0, 0))
    return pl.pallas_call(
        functools.partial(_inproj_sample_kernel, seq),
        grid=(1,),
        in_specs=[full(D_MODEL)] + _inproj_weight_specs() + [full(D_CONV), full(D_CONV)],
        out_specs=specs,
        out_shape=shapes,
        compiler_params=pltpu.CompilerParams(
            dimension_semantics=("arbitrary",), vmem_limit_bytes=VMEM_LIMIT_BYTES),
        name="inproj_sample",
    )(x, n1, win, convw, wpc, p1, p2)


def _stick_breaking_step(logits_fns, value_fns, tri, masks, acc_ref, r_ref):
    zs = [f() for f in logits_fns]
    sums = [_dot(_softplus2_operand(z, m), tri) for z, m in zip(zs, masks)]
    r = r_ref[...]
    acc = acc_ref[...]
    for z, cs, m, value_fn in zip(zs, sums, masks, value_fns):
        acc = acc + value_fn(_weights(z, cs, r, m))
        r = r + _row_total(cs)
    acc_ref[...] = acc
    r_ref[...] = r


def _prompt_attn_kernel(bias_ref, q_ref, k_ref, v_ref, tri_ref, o_ref,
                        acc_ref, r_ref, z_buf, c_buf, t_buf):
    tq, tk = ATT_Q_TILE, ATT_K_TILE
    pair = pl.program_id(1)
    i = pl.program_id(2)

    qq = q_ref[0]
    lane = lax.broadcasted_iota(jnp.int32, qq.shape, 1)
    zero = jnp.zeros_like(qq)
    qs = jnp.concatenate([jnp.where(lane < HEAD_DIM, qq, zero),
                          jnp.where(lane >= HEAD_DIM, qq, zero)], axis=0)
    row = lax.broadcasted_iota(jnp.int32, (2 * tq, 1), 0)
    bias = jnp.where(row < tq, bias_ref[2 * pair], bias_ref[2 * pair + 1])

    acc_ref[...] = jnp.zeros_like(acc_ref)
    r_ref[...] = jnp.zeros_like(r_ref)
    tri = tri_ref[...]

    def keys(j):
        return k_ref[0, pl.ds(pl.multiple_of(j * tk, tk), tk), :]

    def values(j):
        return v_ref[0, pl.ds(pl.multiple_of(j * tk, tk), tk), :]

    def logits(j):
        return _dot_nt(qs, keys(j)) + bias

    odd = i & 1
    n_pairs = lax.shift_right_logical(i, 1)
    top = i - 1 - odd

    def pair(m):
        return [top - 2 * m - sub for sub in range(2)]

    def finish_scores(zs, s):
        for sub, z in enumerate(zs):
            z_buf[s, sub] = z
            cs = _dot(_softplus2_operand(z, None), tri)
            c_buf[s, sub] = cs
            t_buf[s, sub] = _row_total(cs)

    def accumulate(m, s):
        r = r_ref[...]
        acc = acc_ref[...]
        for sub, j in enumerate(pair(m)):
            w = _weights(z_buf[s, sub], c_buf[s, sub], r, None)
            acc = acc + _dot(w, values(j))
            r = r + t_buf[s, sub]
        acc_ref[...] = acc
        r_ref[...] = r

    def step(m, s):
        zs = [logits(j) for j in pair(m)]
        accumulate(m - 1, 1 - s)
        finish_scores(zs, s)

    def head_step(js, masks):
        zs = [logits(jnp.maximum(j, 0)) for j in pair(0)]
        _stick_breaking_step([functools.partial(logits, j) for j in js],
                             [lambda w, j=j: _dot(w, values(j)) for j in js],
                             tri, masks, acc_ref, r_ref)
        finish_scores(zs, 0)

    qpos = lax.broadcasted_iota(jnp.int32, (2 * tq, tk), 0) % tq
    kpos = lax.broadcasted_iota(jnp.int32, (2 * tq, tk), 1)
    causal = kpos < qpos

    @pl.when(odd == 0)
    def _():
        head_step([i], [causal])

    @pl.when(odd == 1)
    def _():
        head_step([i, i - 1], [causal, None])

    @pl.when(n_pairs >= 1)
    def _():
        def body(u, carry):
            t = 1 + 2 * u
            step(t, 1)
            step(t + 1, 0)
            return carry

        lax.fori_loop(0, lax.shift_right_logical(n_pairs - 1, 1), body, 0)
        last = n_pairs - 1

        @pl.when((last & 1) == 1)
        def _():
            step(last, 1)
            accumulate(last, 1)

        @pl.when((last & 1) == 0)
        def _():
            accumulate(last, 0)

    acc = acc_ref[...]
    lane_o = lax.broadcasted_iota(jnp.int32, (tq, 2 * HEAD_DIM), 1)
    o_ref[0] = jnp.where(lane_o < HEAD_DIM, acc[:tq], acc[tq:]).astype(o_ref.dtype)


def _prompt_attn(q, k, v, bias2):
    b, s, _ = q.shape
    tq, tk = ATT_Q_TILE, ATT_K_TILE
    assert tq == tk and s % tq == 0
    kv_spec = pl.BlockSpec((1, s, 2 * HEAD_DIM), lambda bb, p, i: (bb, 0, p))
    return pl.pallas_call(
        _prompt_attn_kernel,
        grid=(b, N_HEADS // 2, s // tq),
        in_specs=[pl.BlockSpec(memory_space=pltpu.SMEM),
                  pl.BlockSpec((1, tq, 2 * HEAD_DIM), lambda bb, p, i: (bb, i, p)),
                  kv_spec, kv_spec,
                  _const_spec((tk, tk))],
        out_specs=pl.BlockSpec((1, tq, 2 * HEAD_DIM), lambda bb, p, i: (bb, i, p)),
        out_shape=jax.ShapeDtypeStruct((b, s, D_ATT), BF16),
        scratch_shapes=[pltpu.VMEM((2 * tq, 2 * HEAD_DIM), F32),
                        pltpu.VMEM((2 * tq, LANES), F32),
                        pltpu.VMEM((2, 2, 2 * tq, tk), F32),
                        pltpu.VMEM((2, 2, 2 * tq, tk), F32),
                        pltpu.VMEM((2, 2, 2 * tq, LANES), F32)],
        compiler_params=pltpu.CompilerParams(
            dimension_semantics=("arbitrary", "arbitrary", "arbitrary"),
            vmem_limit_bytes=VMEM_LIMIT_BYTES),
        name="prompt_attn",
    )(bias2, q, k, v, _tri(tk))


def _sample_attn_kernel(pt_ref, qbd_ref, knt_ref, vnt_ref, tri_ref, bias_ref, *refs):
    del pt_ref
    npg = SAMPLE_PAGES_PER_STEP
    k_refs, v_refs = refs[:npg], refs[npg:2 * npg]
    o_ref, acc_ref, r_ref = refs[2 * npg:]
    n = pl.program_id(1)
    hq = qbd_ref.shape[1]
    nq = hq // N_HEADS
    qbd = qbd_ref[0]
    bias = bias_ref[...]
    tri = tri_ref[...]

    def step(pages, mask):
        logits_fns = [lambda kt=kt: _dot(qbd, kt().astype(BF16)) + bias for kt, _ in pages]
        value_fns = [lambda w, vt=vt: _dot_nt(w, vt().astype(BF16)) for _, vt in pages]
        _stick_breaking_step(logits_fns, value_fns, tri, [mask] * len(pages), acc_ref, r_ref)

    @pl.when(n == 0)
    def _():
        acc_ref[...] = jnp.zeros_like(acc_ref)
        r_ref[...] = jnp.zeros_like(r_ref)
        kpos = lax.broadcasted_iota(jnp.int32, (hq, PAGE_SIZE), 1)
        qpos = lax.broadcasted_iota(jnp.int32, (hq, PAGE_SIZE), 0) % nq
        step([(lambda: knt_ref[0], lambda: vnt_ref[0])], kpos < qpos)

    def page(ref):
        return lambda: ref[0, 0].reshape(N_HEADS * HEAD_DIM, PAGE_SIZE)

    step([(page(k), page(v)) for k, v in zip(k_refs, v_refs)], None)

    @pl.when(n == pl.num_programs(1) - 1)
    def _():
        acc = acc_ref[...]
        for h in range(N_HEADS):
            o_ref[0, h * nq:(h + 1) * nq, :] = acc[h * nq:(h + 1) * nq,
                                                   h * HEAD_DIM:(h + 1) * HEAD_DIM]


def _sample_attn(page_table, qbd, knt, vnt, cache_kt, cache_vt, bias_col):
    bd, n_pages = page_table.shape
    hq = qbd.shape[1]
    npg = SAMPLE_PAGES_PER_STEP
    assert n_pages % npg == 0

    def page_spec(p):
        return pl.BlockSpec(
            (1, 1, N_HEADS, HEAD_DIM, PAGE_SIZE),
            lambda b, n, pt: (0, pt[b, n_pages - 1 - (n * npg + p)], 0, 0, 0))

    per_req = lambda shape: pl.BlockSpec((1,) + shape, lambda b, n, pt: (b, 0, 0))
    const = lambda shape: pl.BlockSpec(shape, lambda b, n, pt: (0, 0))
    pages = [page_spec(p) for p in range(npg)]
    grid_spec = pltpu.PrefetchScalarGridSpec(
        num_scalar_prefetch=1,
        grid=(bd, n_pages // npg),
        in_specs=[per_req((hq, D_ATT)), per_req((D_ATT, PAGE_SIZE)), per_req((D_ATT, PAGE_SIZE)),
                  const((PAGE_SIZE, PAGE_SIZE)), const((hq, 1))] + pages + pages,
        out_specs=per_req((hq, HEAD_DIM)),
        scratch_shapes=[pltpu.VMEM((hq, D_ATT), F32), pltpu.VMEM((hq, LANES), F32)],
    )
    return pl.pallas_call(
        _sample_attn_kernel,
        grid_spec=grid_spec,
        out_shape=jax.ShapeDtypeStruct((bd, hq, HEAD_DIM), F32),
        compiler_params=pltpu.CompilerParams(
            dimension_semantics=("arbitrary", "arbitrary"), vmem_limit_bytes=VMEM_LIMIT_BYTES),
        name="sample_attn",
    )(page_table, qbd, knt, vnt, _tri(PAGE_SIZE), bias_col,
      *([cache_kt] * npg), *([cache_vt] * npg))


def _output_kernel(x_ref, o_ref, ma_ref, gb_ref, wpa_ref, wout_ref, n2_ref, wup_ref, wdown_ref,
                   nf_ref, y_ref):
    y_att = _dot(o_ref[...], wpa_ref[...])
    merged = ma_ref[...] + gb_ref[...] * y_att
    x1 = x_ref[...] + _dot(merged.astype(BF16), wout_ref[...])
    h2 = _rmsnorm(x1, n2_ref[...]).astype(BF16)
    x2 = x1
    ff_chunk = D_MODEL
    for c in range(0, D_FF, ff_chunk):
        f = jnp.maximum(_dot(h2, wup_ref[:, c:c + ff_chunk]), 0.0)
        x2 = x2 + _dot((f * f).astype(BF16), wdown_ref[c:c + ff_chunk, :])
    y_ref[...] = _rmsnorm(x2, nf_ref[...])


def _output(x, o, ma, gb, wpa, wout, n2, wup, wdown, nf, tm):
    n = x.shape[0]
    row = lambda d: pl.BlockSpec((tm, d), lambda i: (i, 0))
    return pl.pallas_call(
        _output_kernel,
        grid=(n // tm,),
        in_specs=[row(D_MODEL), row(D_ATT), row(D_MODEL), row(D_MODEL),
                  _const_spec((D_ATT, D_MODEL)), _const_spec((D_MODEL, D_MODEL)),
                  _const_spec((1, D_MODEL)), _const_spec((D_MODEL, D_FF)),
                  _const_spec((D_FF, D_MODEL)), _const_spec((1, D_MODEL))],
        out_specs=row(D_MODEL),
        out_shape=jax.ShapeDtypeStruct((n, D_MODEL), F32),
        compiler_params=pltpu.CompilerParams(
            dimension_semantics=("arbitrary",), vmem_limit_bytes=VMEM_LIMIT_BYTES),
        name="output_mlp",
    )(x, o, ma, gb, wpa, wout, n2, wup, wdown, nf)


def kernel(x_prompt, x_sample, cache_k, cache_v, state_conv, page_table, norm1_w, w_in, conv_w,
           w_proj_conv, w_proj_attn, att_bias, w_out, norm2_w, w_up, w_down, norm_f_w):
    depth = w_in.shape[0]
    assert depth == 1, "single-layer stack"
    bp, s, _ = x_prompt.shape
    bd, ls, _ = x_sample.shape

    n1 = norm1_w[0][None, :]
    n2 = norm2_w[0][None, :]
    nf = norm_f_w[None, :]
    win = w_in[0].astype(BF16)
    wpc = w_proj_conv[0].astype(BF16)
    wpa = w_proj_attn[0].astype(BF16)
    wout = w_out[0].astype(BF16)
    wup = w_up[0].astype(BF16)
    wdown = w_down[0].astype(BF16)
    convw = conv_w[0]
    bias2 = att_bias[0] * LOG2E

    xp = x_prompt.reshape(bp * s, D_MODEL)
    qp, kbp, vbp, kp, vp, cup, map_, gbp = _inproj_prompt(xp, n1, win, convw, wpc, s)
    op = _prompt_attn(qp.reshape(bp, s, D_ATT), kbp.reshape(bp, s, D_ATT),
                      vbp.reshape(bp, s, D_ATT), bias2)
    yp = _output(xp, op.reshape(bp * s, D_ATT), map_, gbp, wpa, wout, n2, wup, wdown, nf,
                 TOKEN_TILE)

    xs = x_sample.reshape(bd * ls, D_MODEL)
    st = state_conv[0]
    zeros = jnp.zeros((bd, ls, D_CONV), F32)
    p1 = zeros.at[:, 0].set(st[:, 1]).reshape(bd * ls, D_CONV)
    p2 = zeros.at[:, 0].set(st[:, 0]).at[:, 1].set(st[:, 1]).reshape(bd * ls, D_CONV)
    qs, _, _, ks, vs, cus, mas, gbs = _inproj_sample(xs, n1, win, convw, wpc, p1, p2, ls)
    hq = N_HEADS * ls
    q4 = qs.reshape(bd, ls, N_HEADS, HEAD_DIM).transpose(0, 2, 1, 3)
    eye = jnp.eye(N_HEADS, dtype=BF16)
    qbd = (q4[:, :, :, None, :] * eye[None, :, None, :, None]).reshape(bd, hq, D_ATT)
    bias_col = jnp.repeat(bias2, ls)[:, None]
    pad = ((0, 0), (0, 0), (0, PAGE_SIZE - ls))
    knt = jnp.pad(ks.reshape(bd, ls, D_ATT).transpose(0, 2, 1), pad)
    vnt = jnp.pad(vs.reshape(bd, ls, D_ATT).transpose(0, 2, 1), pad)
    cache_kt = cache_k.transpose(0, 1, 3, 4, 2)
    cache_vt = cache_v.transpose(0, 1, 3, 4, 2)
    os_ = _sample_attn(page_table, qbd, knt, vnt, cache_kt, cache_vt, bias_col)
    os_ = os_.reshape(bd, N_HEADS, ls, HEAD_DIM).transpose(0, 2, 1, 3).reshape(bd * ls, D_ATT)
    ys = _output(xs, os_.astype(BF16), mas, gbs, wpa, wout, n2, wup, wdown, nf, bd * ls)

    return (
        yp.reshape(bp, s, D_MODEL),
        ys.reshape(bd, ls, D_MODEL),
        kp.reshape(depth, bp, s // PAGE_SIZE, PAGE_SIZE, N_HEADS, HEAD_DIM),
        vp.reshape(depth, bp, s // PAGE_SIZE, PAGE_SIZE, N_HEADS, HEAD_DIM),
        cup.reshape(bp, s, D_CONV)[None, :, s - 2:, :],
        ks.reshape(depth, bd, ls, N_HEADS, HEAD_DIM),
        vs.reshape(depth, bd, ls, N_HEADS, HEAD_DIM),
        cus.reshape(bd, ls, D_CONV)[None, :, ls - 2:, :],
    )
```

```python
import functools
import math

import jax
import jax.numpy as jnp
from jax import lax
from jax.experimental import pallas as pl
from jax.experimental.pallas import tpu as pltpu

D_MODEL = 1024
D_CONV = 512
N_HEADS = 8
HEAD_DIM = 64
D_ATT = N_HEADS * HEAD_DIM
D_FF = 4 * D_MODEL
D_IN = 3 * D_CONV + 3 * D_ATT + 2 * D_MODEL
PAGE_SIZE = 128
LANES = 128
RMS_EPS = 1e-6
LOG2E = math.log2(math.e)

VMEM_LIMIT_BYTES = 56 * 1024 * 1024

TOKEN_TILE = 512
ATT_Q_TILE = 512
ATT_K_TILE = 256
SAMPLE_PAGES_PER_STEP = 16

F32 = jnp.float32
BF16 = jnp.bfloat16


def _dot(a, b):
    return jnp.dot(a, b, preferred_element_type=F32)


def _dot_nt(a, b):
    return lax.dot_general(a, b, (((1,), (1,)), ((), ())), preferred_element_type=F32)


def _rmsnorm(x, w):
    return x * lax.rsqrt(jnp.mean(x * x, axis=-1, keepdims=True) + RMS_EPS) * w


def _softplus2(z):
    neg_abs = lax.bitcast_convert_type(
        lax.bitcast_convert_type(z, jnp.uint32) | jnp.uint32(0x80000000), F32)
    return jnp.maximum(z, 0.0) + jnp.log(1.0 + jnp.exp2(neg_abs)) * LOG2E


def _softplus2_operand(z, mask):
    sp = _softplus2(z)
    if mask is not None:
        sp = jnp.where(mask, sp, 0.0)
    return sp.astype(BF16)


def _weights(z, cs, r, mask):
    r_full = jnp.concatenate([r] * (z.shape[1] // LANES), axis=1) if z.shape[1] > LANES else r
    w = jnp.exp2(z - cs - r_full)
    if mask is not None:
        w = jnp.where(mask, w, 0.0)
    return w.astype(BF16)


def _row_total(cs):
    return jnp.broadcast_to(cs[:, 0:1], (cs.shape[0], LANES))


def _tri(n):
    j = lax.broadcasted_iota(jnp.int32, (n, n), 0)
    s = lax.broadcasted_iota(jnp.int32, (n, n), 1)
    return (j >= s).astype(BF16)


def _const_spec(shape):
    return pl.BlockSpec(shape, lambda *_: (0,) * len(shape))


def _inproj_body(x_ref, n1_ref, win_ref, convw_ref, wpc_ref, prev_fn,
                 q_ref, kb_ref, vb_ref, k_ref, v_ref, cu_ref, ma_ref, gb_ref):
    x = x_ref[...]
    hb = _rmsnorm(x, n1_ref[...]).astype(BF16)

    def proj(lo, hi):
        return _dot(hb, win_ref[:, lo:hi])

    o = 0
    b_g = proj(o, o + D_CONV); o += D_CONV
    c_g = proj(o, o + D_CONV); o += D_CONV
    u = proj(o, o + D_CONV); o += D_CONV
    cu = c_g * u
    cu_ref[...] = cu
    prev1, prev2 = prev_fn(cu)
    conv = convw_ref[0:1, :] * prev2 + convw_ref[1:2, :] * prev1 + convw_ref[2:3, :] * cu
    y_conv = _dot((b_g * conv).astype(BF16), wpc_ref[...])

    q = proj(o, o + D_ATT); o += D_ATT
    q_ref[...] = (q * (LOG2E * HEAD_DIM ** -0.5)).astype(BF16)
    k = proj(o, o + D_ATT); o += D_ATT
    _store_kv(k_ref, k)
    kb_ref[...] = k.astype(BF16)
    v = proj(o, o + D_ATT); o += D_ATT
    _store_kv(v_ref, v)
    vb_ref[...] = v.astype(BF16)
    gate_conv = proj(o, o + D_MODEL); o += D_MODEL
    ma_ref[...] = jax.nn.sigmoid(gate_conv) * y_conv
    gate_att = proj(o, o + D_MODEL); o += D_MODEL
    gb_ref[...] = jax.nn.sigmoid(gate_att)


def _inproj_prompt_kernel(tiles_per_seq, x_ref, n1_ref, win_ref, convw_ref, wpc_ref,
                          q_ref, kb_ref, vb_ref, k_ref, v_ref, cu_ref, ma_ref, gb_ref, carry_ref):
    tm = x_ref.shape[0]

    @pl.when(pl.program_id(0) % tiles_per_seq == 0)
    def _():
        carry_ref[...] = jnp.zeros_like(carry_ref)

    def prev_fn(cu):
        row = lax.broadcasted_iota(jnp.int32, cu.shape, 0)
        c1 = carry_ref[7:8, :]
        c2 = carry_ref[6:7, :]
        prev1 = jnp.where(row == 0, c1, pltpu.roll(cu, 1, 0))
        prev2 = jnp.where(row == 0, c2, jnp.where(row == 1, c1, pltpu.roll(cu, 2, 0)))
        carry_ref[...] = cu[tm - 8:, :]
        return prev1, prev2

    _inproj_body(x_ref, n1_ref, win_ref, convw_ref, wpc_ref, prev_fn,
                 q_ref, kb_ref, vb_ref, k_ref, v_ref, cu_ref, ma_ref, gb_ref)


def _inproj_sample_kernel(seq, x_ref, n1_ref, win_ref, convw_ref, wpc_ref, p1_ref, p2_ref,
                          q_ref, kb_ref, vb_ref, k_ref, v_ref, cu_ref, ma_ref, gb_ref):
    def prev_fn(cu):
        pos = lax.broadcasted_iota(jnp.int32, cu.shape, 0) % seq
        prev1 = jnp.where(pos == 0, p1_ref[...], pltpu.roll(cu, 1, 0))
        prev2 = jnp.where(pos < 2, p2_ref[...], pltpu.roll(cu, 2, 0))
        return prev1, prev2

    _inproj_body(x_ref, n1_ref, win_ref, convw_ref, wpc_ref, prev_fn,
                 q_ref, kb_ref, vb_ref, k_ref, v_ref, cu_ref, ma_ref, gb_ref)


def _store_kv(ref, x):
    if len(ref.shape) == 2:
        ref[...] = x
    else:
        for p in range(ref.shape[0]):
            ref[p] = x[p * PAGE_SIZE:(p + 1) * PAGE_SIZE, :].T


def _inproj_out(n, tm, paged_kv):
    row = lambda d: pl.BlockSpec((tm, d), lambda i: (i, 0))
    if paged_kv:
        kv_shape = jax.ShapeDtypeStruct((n // PAGE_SIZE, D_ATT, PAGE_SIZE), F32)
        kv_spec = pl.BlockSpec((tm // PAGE_SIZE, D_ATT, PAGE_SIZE), lambda i: (i, 0, 0))
    else:
        kv_shape = jax.ShapeDtypeStruct((n, D_ATT), F32)
        kv_spec = row(D_ATT)
    shapes = [
        jax.ShapeDtypeStruct((n, D_ATT), BF16),
        jax.ShapeDtypeStruct((n, D_ATT), BF16),
        jax.ShapeDtypeStruct((n, D_ATT), BF16),
        kv_shape,
        kv_shape,
        jax.ShapeDtypeStruct((n, D_CONV), F32),
        jax.ShapeDtypeStruct((n, D_MODEL), F32),
        jax.ShapeDtypeStruct((n, D_MODEL), F32),
    ]
    specs = [row(D_ATT)] * 3 + [kv_spec] * 2 + [row(D_CONV), row(D_MODEL), row(D_MODEL)]
    return shapes, specs


def _inproj_weight_specs():
    return [_const_spec((1, D_MODEL)), _const_spec((D_MODEL, D_IN)),
            _const_spec((3, D_CONV)), _const_spec((D_CONV, D_MODEL))]


def _inproj_prompt(x, n1, win, convw, wpc, seq_len):
    n = x.shape[0]
    tm = TOKEN_TILE
    shapes, specs = _inproj_out(n, tm, paged_kv=True)
    return pl.pallas_call(
        functools.partial(_inproj_prompt_kernel, seq_len // tm),
        grid=(n // tm,),
        in_specs=[pl.BlockSpec((tm, D_MODEL), lambda i: (i, 0))] + _inproj_weight_specs(),
        out_specs=specs,
        out_shape=shapes,
        scratch_shapes=[pltpu.VMEM((8, D_CONV), F32)],
        compiler_params=pltpu.CompilerParams(
            dimension_semantics=("arbitrary",), vmem_limit_bytes=VMEM_LIMIT_BYTES),
        name="inproj_prompt",
    )(x, n1, win, convw, wpc)


def _inproj_sample(x, n1, win, convw, wpc, p1, p2, seq):
    n = x.shape[0]
    shapes, specs = _inproj_out(n, n, paged_kv=False)
    full = lambda d: pl.BlockSpec((n, d), lambda i: (0, 0))
    return pl.pallas_call(
        functools.partial(_inproj_sample_kernel, seq),
        grid=(1,),
        in_specs=[full(D_MODEL)] + _inproj_weight_specs() + [full(D_CONV), full(D_CONV)],
        out_specs=specs,
        out_shape=shapes,
        compiler_params=pltpu.CompilerParams(
            dimension_semantics=("arbitrary",), vmem_limit_bytes=VMEM_LIMIT_BYTES),
        name="inproj_sample",
    )(x, n1, win, convw, wpc, p1, p2)


def _stick_breaking_step(logits_fns, value_fns, tri, masks, acc_ref, r_ref):
    zs = [f() for f in logits_fns]
    sums = [_dot(_softplus2_operand(z, m), tri) for z, m in zip(zs, masks)]
    r = r_ref[...]
    acc = acc_ref[...]
    for z, cs, m, value_fn in zip(zs, sums, masks, value_fns):
        acc = acc + value_fn(_weights(z, cs, r, m))
        r = r + _row_total(cs)
    acc_ref[...] = acc
    r_ref[...] = r


def _prompt_attn_kernel(bias_ref, q_ref, k_ref, v_ref, tri_ref, o_ref,
                        acc_ref, r_ref, z_buf, c_buf, t_buf):
    tq, tk = ATT_Q_TILE, ATT_K_TILE
    pair = pl.program_id(1)
    i = pl.program_id(2)

    qq = q_ref[0]
    lane = lax.broadcasted_iota(jnp.int32, qq.shape, 1)
    zero = jnp.zeros_like(qq)
    qs = jnp.concatenate([jnp.where(lane < HEAD_DIM, qq, zero),
                          jnp.where(lane >= HEAD_DIM, qq, zero)], axis=0)
    row = lax.broadcasted_iota(jnp.int32, (2 * tq, 1), 0)
    bias = jnp.where(row < tq, bias_ref[2 * pair], bias_ref[2 * pair + 1])

    acc_ref[...] = jnp.zeros_like(acc_ref)
    r_ref[...] = jnp.zeros_like(r_ref)
    tri = tri_ref[...]

    def keys(j):
        return k_ref[0, pl.ds(pl.multiple_of(j * tk, tk), tk), :]

    def values(j):
        return v_ref[0, pl.ds(pl.multiple_of(j * tk, tk), tk), :]

    def logits(j):
        return _dot_nt(qs, keys(j)) + bias

    n_diag = tq // tk
    first_diag = i * n_diag
    odd = first_diag & 1
    n_pairs = lax.shift_right_logical(first_diag, 1)
    top = first_diag - 1 - odd

    def pair(m):
        return [top - 2 * m - sub for sub in range(2)]

    def finish_scores(zs, s):
        for sub, z in enumerate(zs):
            z_buf[s, sub] = z
            cs = _dot(_softplus2_operand(z, None), tri)
            c_buf[s, sub] = cs
            t_buf[s, sub] = _row_total(cs)

    def accumulate(m, s):
        r = r_ref[...]
        acc = acc_ref[...]
        for sub, j in enumerate(pair(m)):
            w = _weights(z_buf[s, sub], c_buf[s, sub], r, None)
            acc = acc + _dot(w, values(j))
            r = r + t_buf[s, sub]
        acc_ref[...] = acc
        r_ref[...] = r

    def step(m, s):
        zs = [logits(j) for j in pair(m)]
        accumulate(m - 1, 1 - s)
        finish_scores(zs, s)

    def head_step(js, masks):
        zs = [logits(jnp.maximum(j, 0)) for j in pair(0)]
        _stick_breaking_step([functools.partial(logits, j) for j in js],
                             [lambda w, j=j: _dot(w, values(j)) for j in js],
                             tri, masks, acc_ref, r_ref)
        finish_scores(zs, 0)

    qpos = lax.broadcasted_iota(jnp.int32, (2 * tq, tk), 0) % tq
    kpos = lax.broadcasted_iota(jnp.int32, (2 * tq, tk), 1)
    diag = [first_diag + d for d in reversed(range(n_diag))]
    causal = [kpos + d * tk < qpos for d in reversed(range(n_diag))]

    if n_diag % 2 == 0:
        head_step(diag, causal)
    else:
        @pl.when(odd == 0)
        def _():
            head_step(diag, causal)

        @pl.when(odd == 1)
        def _():
            head_step(diag + [first_diag - 1], causal + [None])

    @pl.when(n_pairs >= 1)
    def _():
        def body(u, carry):
            t = 1 + 2 * u
            step(t, 1)
            step(t + 1, 0)
            return carry

        lax.fori_loop(0, lax.shift_right_logical(n_pairs - 1, 1), body, 0)
        last = n_pairs - 1

        @pl.when((last & 1) == 1)
        def _():
            step(last, 1)
            accumulate(last, 1)

        @pl.when((last & 1) == 0)
        def _():
            accumulate(last, 0)

    acc = acc_ref[...]
    lane_o = lax.broadcasted_iota(jnp.int32, (tq, 2 * HEAD_DIM), 1)
    o_ref[0] = jnp.where(lane_o < HEAD_DIM, acc[:tq], acc[tq:]).astype(o_ref.dtype)


def _prompt_attn(q, k, v, bias2):
    b, s, _ = q.shape
    tq, tk = ATT_Q_TILE, ATT_K_TILE
    assert tq % tk == 0 and s % tq == 0
    kv_spec = pl.BlockSpec((1, s, 2 * HEAD_DIM), lambda bb, p, i: (bb, 0, p))
    return pl.pallas_call(
        _prompt_attn_kernel,
        grid=(b, N_HEADS // 2, s // tq),
        in_specs=[pl.BlockSpec(memory_space=pltpu.SMEM),
                  pl.BlockSpec((1, tq, 2 * HEAD_DIM), lambda bb, p, i: (bb, i, p)),
                  kv_spec, kv_spec,
                  _const_spec((tk, tk))],
        out_specs=pl.BlockSpec((1, tq, 2 * HEAD_DIM), lambda bb, p, i: (bb, i, p)),
        out_shape=jax.ShapeDtypeStruct((b, s, D_ATT), BF16),
        scratch_shapes=[pltpu.VMEM((2 * tq, 2 * HEAD_DIM), F32),
                        pltpu.VMEM((2 * tq, LANES), F32),
                        pltpu.VMEM((2, 2, 2 * tq, tk), F32),
                        pltpu.VMEM((2, 2, 2 * tq, tk), F32),
                        pltpu.VMEM((2, 2, 2 * tq, LANES), F32)],
        compiler_params=pltpu.CompilerParams(
            dimension_semantics=("arbitrary", "arbitrary", "arbitrary"),
            vmem_limit_bytes=VMEM_LIMIT_BYTES),
        name="prompt_attn",
    )(bias2, q, k, v, _tri(tk))


def _sample_attn_kernel(pt_ref, qbd_ref, knt_ref, vnt_ref, tri_ref, bias_ref, *refs):
    del pt_ref
    npg = SAMPLE_PAGES_PER_STEP
    k_refs, v_refs = refs[:npg], refs[npg:2 * npg]
    o_ref, acc_ref, r_ref = refs[2 * npg:]
    n = pl.program_id(1)
    hq = qbd_ref.shape[1]
    nq = hq // N_HEADS
    qbd = qbd_ref[0]
    bias = bias_ref[...]
    tri = tri_ref[...]

    def step(pages, mask):
        logits_fns = [lambda kt=kt: _dot(qbd, kt().astype(BF16)) + bias for kt, _ in pages]
        value_fns = [lambda w, vt=vt: _dot_nt(w, vt().astype(BF16)) for _, vt in pages]
        _stick_breaking_step(logits_fns, value_fns, tri, [mask] * len(pages), acc_ref, r_ref)

    @pl.when(n == 0)
    def _():
        acc_ref[...] = jnp.zeros_like(acc_ref)
        r_ref[...] = jnp.zeros_like(r_ref)
        kpos = lax.broadcasted_iota(jnp.int32, (hq, PAGE_SIZE), 1)
        qpos = lax.broadcasted_iota(jnp.int32, (hq, PAGE_SIZE), 0) % nq
        step([(lambda: knt_ref[0], lambda: vnt_ref[0])], kpos < qpos)

    def page(ref):
        return lambda: ref[0, 0].reshape(N_HEADS * HEAD_DIM, PAGE_SIZE)

    step([(page(k), page(v)) for k, v in zip(k_refs, v_refs)], None)

    @pl.when(n == pl.num_programs(1) - 1)
    def _():
        acc = acc_ref[...]
        for h in range(N_HEADS):
            o_ref[0, h * nq:(h + 1) * nq, :] = acc[h * nq:(h + 1) * nq,
                                                   h * HEAD_DIM:(h + 1) * HEAD_DIM]


def _sample_attn(page_table, qbd, knt, vnt, cache_kt, cache_vt, bias_col):
    bd, n_pages = page_table.shape
    hq = qbd.shape[1]
    npg = SAMPLE_PAGES_PER_STEP
    assert n_pages % npg == 0

    def page_spec(p):
        return pl.BlockSpec(
            (1, 1, N_HEADS, HEAD_DIM, PAGE_SIZE),
            lambda b, n, pt: (0, pt[b, n_pages - 1 - (n * npg + p)], 0, 0, 0))

    per_req = lambda shape: pl.BlockSpec((1,) + shape, lambda b, n, pt: (b, 0, 0))
    const = lambda shape: pl.BlockSpec(shape, lambda b, n, pt: (0, 0))
    pages = [page_spec(p) for p in range(npg)]
    grid_spec = pltpu.PrefetchScalarGridSpec(
        num_scalar_prefetch=1,
        grid=(bd, n_pages // npg),
        in_specs=[per_req((hq, D_ATT)), per_req((D_ATT, PAGE_SIZE)), per_req((D_ATT, PAGE_SIZE)),
                  const((PAGE_SIZE, PAGE_SIZE)), const((hq, 1))] + pages + pages,
        out_specs=per_req((hq, HEAD_DIM)),
        scratch_shapes=[pltpu.VMEM((hq, D_ATT), F32), pltpu.VMEM((hq, LANES), F32)],
    )
    return pl.pallas_call(
        _sample_attn_kernel,
        grid_spec=grid_spec,
        out_shape=jax.ShapeDtypeStruct((bd, hq, HEAD_DIM), F32),
        compiler_params=pltpu.CompilerParams(
            dimension_semantics=("arbitrary", "arbitrary"), vmem_limit_bytes=VMEM_LIMIT_BYTES),
        name="sample_attn",
    )(page_table, qbd, knt, vnt, _tri(PAGE_SIZE), bias_col,
      *([cache_kt] * npg), *([cache_vt] * npg))


def _output_kernel(x_ref, o_ref, ma_ref, gb_ref, wpa_ref, wout_ref, n2_ref, wup_ref, wdown_ref,
                   nf_ref, y_ref):
    y_att = _dot(o_ref[...], wpa_ref[...])
    merged = ma_ref[...] + gb_ref[...] * y_att
    x1 = x_ref[...] + _dot(merged.astype(BF16), wout_ref[...])
    h2 = _rmsnorm(x1, n2_ref[...]).astype(BF16)
    x2 = x1
    ff_chunk = D_MODEL
    for c in range(0, D_FF, ff_chunk):
        f = jnp.maximum(_dot(h2, wup_ref[:, c:c + ff_chunk]), 0.0)
        x2 = x2 + _dot((f * f).astype(BF16), wdown_ref[c:c + ff_chunk, :])
    y_ref[...] = _rmsnorm(x2, nf_ref[...])


def _output(x, o, ma, gb, wpa, wout, n2, wup, wdown, nf, tm):
    n = x.shape[0]
    row = lambda d: pl.BlockSpec((tm, d), lambda i: (i, 0))
    return pl.pallas_call(
        _output_kernel,
        grid=(n // tm,),
        in_specs=[row(D_MODEL), row(D_ATT), row(D_MODEL), row(D_MODEL),
                  _const_spec((D_ATT, D_MODEL)), _const_spec((D_MODEL, D_MODEL)),
                  _const_spec((1, D_MODEL)), _const_spec((D_MODEL, D_FF)),
                  _const_spec((D_FF, D_MODEL)), _const_spec((1, D_MODEL))],
        out_specs=row(D_MODEL),
        out_shape=jax.ShapeDtypeStruct((n, D_MODEL), F32),
        compiler_params=pltpu.CompilerParams(
            dimension_semantics=("arbitrary",), vmem_limit_bytes=VMEM_LIMIT_BYTES),
        name="output_mlp",
    )(x, o, ma, gb, wpa, wout, n2, wup, wdown, nf)


def kernel(x_prompt, x_sample, cache_k, cache_v, state_conv, page_table, norm1_w, w_in, conv_w,
           w_proj_conv, w_proj_attn, att_bias, w_out, norm2_w, w_up, w_down, norm_f_w):
    depth = w_in.shape[0]
    assert depth == 1, "single-layer stack"
    bp, s, _ = x_prompt.shape
    bd, ls, _ = x_sample.shape

    n1 = norm1_w[0][None, :]
    n2 = norm2_w[0][None, :]
    nf = norm_f_w[None, :]
    win = w_in[0].astype(BF16)
    wpc = w_proj_conv[0].astype(BF16)
    wpa = w_proj_attn[0].astype(BF16)
    wout = w_out[0].astype(BF16)
    wup = w_up[0].astype(BF16)
    wdown = w_down[0].astype(BF16)
    convw = conv_w[0]
    bias2 = att_bias[0] * LOG2E

    xp = x_prompt.reshape(bp * s, D_MODEL)
    qp, kbp, vbp, kp, vp, cup, map_, gbp = _inproj_prompt(xp, n1, win, convw, wpc, s)
    op = _prompt_attn(qp.reshape(bp, s, D_ATT), kbp.reshape(bp, s, D_ATT),
                      vbp.reshape(bp, s, D_ATT), bias2)
    yp = _output(xp, op.reshape(bp * s, D_ATT), map_, gbp, wpa, wout, n2, wup, wdown, nf,
                 TOKEN_TILE)

    xs = x_sample.reshape(bd * ls, D_MODEL)
    st = state_conv[0]
    zeros = jnp.zeros((bd, ls, D_CONV), F32)
    p1 = zeros.at[:, 0].set(st[:, 1]).reshape(bd * ls, D_CONV)
    p2 = zeros.at[:, 0].set(st[:, 0]).at[:, 1].set(st[:, 1]).reshape(bd * ls, D_CONV)
    qs, _, _, ks, vs, cus, mas, gbs = _inproj_sample(xs, n1, win, convw, wpc, p1, p2, ls)
    hq = N_HEADS * ls
    q4 = qs.reshape(bd, ls, N_HEADS, HEAD_DIM).transpose(0, 2, 1, 3)
    eye = jnp.eye(N_HEADS, dtype=BF16)
    qbd = (q4[:, :, :, None, :] * eye[None, :, None, :, None]).reshape(bd, hq, D_ATT)
    bias_col = jnp.repeat(bias2, ls)[:, None]
    pad = ((0, 0), (0, 0), (0, PAGE_SIZE - ls))
    knt = jnp.pad(ks.reshape(bd, ls, D_ATT).transpose(0, 2, 1), pad)
    vnt = jnp.pad(vs.reshape(bd, ls, D_ATT).transpose(0, 2, 1), pad)
    cache_kt = cache_k.transpose(0, 1, 3, 4, 2)
    cache_vt = cache_v.transpose(0, 1, 3, 4, 2)
    os_ = _sample_attn(page_table, qbd, knt, vnt, cache_kt, cache_vt, bias_col)
    os_ = os_.reshape(bd, N_HEADS, ls, HEAD_DIM).transpose(0, 2, 1, 3).reshape(bd * ls, D_ATT)
    ys = _output(xs, os_.astype(BF16), mas, gbs, wpa, wout, n2, wup, wdown, nf, bd * ls)

    def paged(x):
        x = x.reshape(depth, bp, s // PAGE_SIZE, N_HEADS, HEAD_DIM, PAGE_SIZE)
        return x.transpose(0, 1, 2, 5, 3, 4)

    return (
        yp.reshape(bp, s, D_MODEL),
        ys.reshape(bd, ls, D_MODEL),
        paged(kp),
        paged(vp),
        cup.reshape(bp, s, D_CONV)[None, :, s - 2:, :],
        ks.reshape(depth, bd, ls, N_HEADS, HEAD_DIM),
        vs.reshape(depth, bd, ls, N_HEADS, HEAD_DIM),
        cus.reshape(bd, ls, D_CONV)[None, :, ls - 2:, :],
    )
```

```python
import functools
import math

import jax
import jax.numpy as jnp
from jax import lax
from jax.experimental import pallas as pl
from jax.experimental.pallas import tpu as pltpu

D_MODEL = 1024
D_CONV = 512
N_HEADS = 8
HEAD_DIM = 64
D_ATT = N_HEADS * HEAD_DIM
D_FF = 4 * D_MODEL
D_IN = 3 * D_CONV + 3 * D_ATT + 2 * D_MODEL
PAGE_SIZE = 128
LANES = 128
RMS_EPS = 1e-6
LOG2E = math.log2(math.e)

VMEM_LIMIT_BYTES = 56 * 1024 * 1024

TOKEN_TILE = 512
ATT_Q_TILE = 512
ATT_K_TILE = 256
SAMPLE_PAGES_PER_STEP = 32

F32 = jnp.float32
BF16 = jnp.bfloat16


def _dot(a, b):
    return jnp.dot(a, b, preferred_element_type=F32)


def _dot_nt(a, b):
    return lax.dot_general(a, b, (((1,), (1,)), ((), ())), preferred_element_type=F32)


def _rmsnorm(x, w):
    return x * lax.rsqrt(jnp.mean(x * x, axis=-1, keepdims=True) + RMS_EPS) * w


def _softplus2(z):
    neg_abs = lax.bitcast_convert_type(
        lax.bitcast_convert_type(z, jnp.uint32) | jnp.uint32(0x80000000), F32)
    return jnp.maximum(z, 0.0) + jnp.log(1.0 + jnp.exp2(neg_abs)) * LOG2E


def _softplus2_operand(z, mask):
    sp = _softplus2(z)
    if mask is not None:
        sp = jnp.where(mask, sp, 0.0)
    return sp.astype(BF16)


def _weights(z, cs, r, mask):
    r_full = jnp.concatenate([r] * (z.shape[1] // LANES), axis=1) if z.shape[1] > LANES else r
    w = jnp.exp2(z - cs - r_full)
    if mask is not None:
        w = jnp.where(mask, w, 0.0)
    return w.astype(BF16)


def _row_total(cs):
    return jnp.broadcast_to(cs[:, 0:1], (cs.shape[0], LANES))


def _tri(n):
    j = lax.broadcasted_iota(jnp.int32, (n, n), 0)
    s = lax.broadcasted_iota(jnp.int32, (n, n), 1)
    return (j >= s).astype(BF16)


def _const_spec(shape):
    return pl.BlockSpec(shape, lambda *_: (0,) * len(shape))


def _inproj_body(x_ref, n1_ref, win_ref, convw_ref, wpc_ref, prev_fn,
                 q_ref, kb_ref, vb_ref, k_ref, v_ref, cu_ref, ma_ref, gb_ref):
    x = x_ref[...]
    hb = _rmsnorm(x, n1_ref[...]).astype(BF16)

    def proj(lo, hi):
        return _dot(hb, win_ref[:, lo:hi])

    o = 0
    b_g = proj(o, o + D_CONV); o += D_CONV
    c_g = proj(o, o + D_CONV); o += D_CONV
    u = proj(o, o + D_CONV); o += D_CONV
    cu = c_g * u
    cu_ref[...] = cu
    prev1, prev2 = prev_fn(cu)
    conv = convw_ref[0:1, :] * prev2 + convw_ref[1:2, :] * prev1 + convw_ref[2:3, :] * cu
    y_conv = _dot((b_g * conv).astype(BF16), wpc_ref[...])

    q = proj(o, o + D_ATT); o += D_ATT
    q_ref[...] = (q * (LOG2E * HEAD_DIM ** -0.5)).astype(BF16)
    k = proj(o, o + D_ATT); o += D_ATT
    _store_kv(k_ref, k)
    kb_ref[...] = k.astype(BF16)
    v = proj(o, o + D_ATT); o += D_ATT
    _store_kv(v_ref, v)
    vb_ref[...] = v.astype(BF16)
    gate_conv = proj(o, o + D_MODEL); o += D_MODEL
    ma_ref[...] = jax.nn.sigmoid(gate_conv) * y_conv
    gate_att = proj(o, o + D_MODEL); o += D_MODEL
    gb_ref[...] = jax.nn.sigmoid(gate_att)


def _inproj_prompt_kernel(tiles_per_seq, x_ref, n1_ref, win_ref, convw_ref, wpc_ref,
                          q_ref, kb_ref, vb_ref, k_ref, v_ref, cu_ref, ma_ref, gb_ref, carry_ref):
    tm = x_ref.shape[0]

    @pl.when(pl.program_id(0) % tiles_per_seq == 0)
    def _():
        carry_ref[...] = jnp.zeros_like(carry_ref)

    def prev_fn(cu):
        row = lax.broadcasted_iota(jnp.int32, cu.shape, 0)
        c1 = carry_ref[7:8, :]
        c2 = carry_ref[6:7, :]
        prev1 = jnp.where(row == 0, c1, pltpu.roll(cu, 1, 0))
        prev2 = jnp.where(row == 0, c2, jnp.where(row == 1, c1, pltpu.roll(cu, 2, 0)))
        carry_ref[...] = cu[tm - 8:, :]
        return prev1, prev2

    _inproj_body(x_ref, n1_ref, win_ref, convw_ref, wpc_ref, prev_fn,
                 q_ref, kb_ref, vb_ref, k_ref, v_ref, cu_ref, ma_ref, gb_ref)


def _inproj_sample_kernel(seq, x_ref, n1_ref, win_ref, convw_ref, wpc_ref, p1_ref, p2_ref,
                          q_ref, kb_ref, vb_ref, k_ref, v_ref, cu_ref, ma_ref, gb_ref):
    def prev_fn(cu):
        pos = lax.broadcasted_iota(jnp.int32, cu.shape, 0) % seq
        prev1 = jnp.where(pos == 0, p1_ref[...], pltpu.roll(cu, 1, 0))
        prev2 = jnp.where(pos < 2, p2_ref[...], pltpu.roll(cu, 2, 0))
        return prev1, prev2

    _inproj_body(x_ref, n1_ref, win_ref, convw_ref, wpc_ref, prev_fn,
                 q_ref, kb_ref, vb_ref, k_ref, v_ref, cu_ref, ma_ref, gb_ref)


def _store_kv(ref, x):
    if len(ref.shape) == 2:
        ref[...] = x
    else:
        for p in range(ref.shape[0]):
            ref[p] = x[p * PAGE_SIZE:(p + 1) * PAGE_SIZE, :].T


def _inproj_out(n, tm, paged_kv):
    row = lambda d: pl.BlockSpec((tm, d), lambda i: (i, 0))
    if paged_kv:
        kv_shape = jax.ShapeDtypeStruct((n // PAGE_SIZE, D_ATT, PAGE_SIZE), F32)
        kv_spec = pl.BlockSpec((tm // PAGE_SIZE, D_ATT, PAGE_SIZE), lambda i: (i, 0, 0))
    else:
        kv_shape = jax.ShapeDtypeStruct((n, D_ATT), F32)
        kv_spec = row(D_ATT)
    shapes = [
        jax.ShapeDtypeStruct((n, D_ATT), BF16),
        jax.ShapeDtypeStruct((n, D_ATT), BF16),
        jax.ShapeDtypeStruct((n, D_ATT), BF16),
        kv_shape,
        kv_shape,
        jax.ShapeDtypeStruct((n, D_CONV), F32),
        jax.ShapeDtypeStruct((n, D_MODEL), F32),
        jax.ShapeDtypeStruct((n, D_MODEL), F32),
    ]
    specs = [row(D_ATT)] * 3 + [kv_spec] * 2 + [row(D_CONV), row(D_MODEL), row(D_MODEL)]
    return shapes, specs


def _inproj_weight_specs():
    return [_const_spec((1, D_MODEL)), _const_spec((D_MODEL, D_IN)),
            _const_spec((3, D_CONV)), _const_spec((D_CONV, D_MODEL))]


def _inproj_prompt(x, n1, win, convw, wpc, seq_len):
    n = x.shape[0]
    tm = TOKEN_TILE
    shapes, specs = _inproj_out(n, tm, paged_kv=True)
    return pl.pallas_call(
        functools.partial(_inproj_prompt_kernel, seq_len // tm),
        grid=(n // tm,),
        in_specs=[pl.BlockSpec((tm, D_MODEL), lambda i: (i, 0))] + _inproj_weight_specs(),
        out_specs=specs,
        out_shape=shapes,
        scratch_shapes=[pltpu.VMEM((8, D_CONV), F32)],
        compiler_params=pltpu.CompilerParams(
            dimension_semantics=("arbitrary",), vmem_limit_bytes=VMEM_LIMIT_BYTES),
        name="inproj_prompt",
    )(x, n1, win, convw, wpc)


def _inproj_sample(x, n1, win, convw, wpc, p1, p2, seq):
    n = x.shape[0]
    shapes, specs = _inproj_out(n, n, paged_kv=False)
    full = lambda d: pl.BlockSpec((n, d), lambda i: (0, 0))
    return pl.pallas_call(
        functools.partial(_inproj_sample_kernel, seq),
        grid=(1,),
        in_specs=[full(D_MODEL)] + _inproj_weight_specs() + [full(D_CONV), full(D_CONV)],
        out_specs=specs,
        out_shape=shapes,
        compiler_params=pltpu.CompilerParams(
            dimension_semantics=("arbitrary",), vmem_limit_bytes=VMEM_LIMIT_BYTES),
        name="inproj_sample",
    )(x, n1, win, convw, wpc, p1, p2)


def _stick_breaking_step(logits_fns, value_fns, tri, masks, acc_ref, r_ref):
    zs = [f() for f in logits_fns]
    sums = [_dot(_softplus2_operand(z, m), tri) for z, m in zip(zs, masks)]
    r = r_ref[...]
    acc = acc_ref[...]
    for z, cs, m, value_fn in zip(zs, sums, masks, value_fns):
        acc = acc + value_fn(_weights(z, cs, r, m))
        r = r + _row_total(cs)
    acc_ref[...] = acc
    r_ref[...] = r


def _prompt_attn_kernel(bias_ref, q_ref, k_ref, v_ref, tri_ref, o_ref,
                        acc_ref, r_ref, z_buf, c_buf, t_buf):
    tq, tk = ATT_Q_TILE, ATT_K_TILE
    pair = pl.program_id(1)
    i = pl.program_id(2)

    qq = q_ref[0]
    lane = lax.broadcasted_iota(jnp.int32, qq.shape, 1)
    zero = jnp.zeros_like(qq)
    qs = jnp.concatenate([jnp.where(lane < HEAD_DIM, qq, zero),
                          jnp.where(lane >= HEAD_DIM, qq, zero)], axis=0)
    row = lax.broadcasted_iota(jnp.int32, (2 * tq, 1), 0)
    bias = jnp.where(row < tq, bias_ref[2 * pair], bias_ref[2 * pair + 1])

    acc_ref[...] = jnp.zeros_like(acc_ref)
    r_ref[...] = jnp.zeros_like(r_ref)
    tri = tri_ref[...]

    def keys(j):
        return k_ref[0, pl.ds(pl.multiple_of(j * tk, tk), tk), :]

    def values(j):
        return v_ref[0, pl.ds(pl.multiple_of(j * tk, tk), tk), :]

    def logits(j):
        return _dot_nt(qs, keys(j)) + bias

    n_diag = tq // tk
    first_diag = i * n_diag
    odd = first_diag & 1
    n_pairs = lax.shift_right_logical(first_diag, 1)
    top = first_diag - 1 - odd

    def pair(m):
        return [top - 2 * m - sub for sub in range(2)]

    def stage_logits(m, zslot):
        for sub, j in enumerate(pair(m)):
            z_buf[zslot, sub] = logits(jnp.maximum(j, 0))

    def stage_sums(zslot, cslot):
        for sub in range(2):
            cs = _dot(_softplus2_operand(z_buf[zslot, sub], None), tri)
            c_buf[cslot, sub] = cs
            t_buf[cslot, sub] = _row_total(cs)

    def stage_values(m, zslot, cslot):
        r = r_ref[...]
        acc = acc_ref[...]
        for sub, j in enumerate(pair(m)):
            w = _weights(z_buf[zslot, sub], c_buf[cslot, sub], r, None)
            acc = acc + _dot(w, values(j))
            r = r + t_buf[cslot, sub]
        acc_ref[...] = acc
        r_ref[...] = r

    def step(t, phase):
        za, zb, zc = (phase + 1) % 4, phase, (phase - 1) % 4
        cb, cc = phase % 2, (phase - 1) % 2
        r = r_ref[...]
        acc = acc_ref[...]
        for sub in range(2):
            w = _weights(z_buf[zc, sub], c_buf[cc, sub], r, None)
            acc = acc + _dot(w, values(pair(t - 1)[sub]))
            r = r + t_buf[cc, sub]
            z_buf[za, sub] = logits(jnp.maximum(pair(t + 1)[sub], 0))
            cs = _dot(_softplus2_operand(z_buf[zb, sub], None), tri)
            c_buf[cb, sub] = cs
            t_buf[cb, sub] = _row_total(cs)
        acc_ref[...] = acc
        r_ref[...] = r

    def head_step(js, masks):
        stage_logits(0, 0)
        stage_logits(1, 1)
        _stick_breaking_step([functools.partial(logits, j) for j in js],
                             [lambda w, j=j: _dot(w, values(j)) for j in js],
                             tri, masks, acc_ref, r_ref)
        stage_sums(0, 0)

    qpos = lax.broadcasted_iota(jnp.int32, (2 * tq, tk), 0) % tq
    kpos = lax.broadcasted_iota(jnp.int32, (2 * tq, tk), 1)
    diag = [first_diag + d for d in reversed(range(n_diag))]
    causal = [kpos + d * tk < qpos for d in reversed(range(n_diag))]

    if n_diag % 2 == 0:
        head_step(diag, causal)
    else:
        @pl.when(odd == 0)
        def _():
            head_step(diag, causal)

        @pl.when(odd == 1)
        def _():
            head_step(diag + [first_diag - 1], causal + [None])

    @pl.when(n_pairs >= 1)
    def _():
        last = n_pairs - 1
        n_quads = lax.shift_right_logical(last, 2)

        def steps(t, count):
            for d in range(count):
                step(t + d, (1 + d) % 4)

        def body(u, carry):
            steps(1 + 4 * u, 4)
            return carry

        lax.fori_loop(0, n_quads, body, 0)
        for rem in range(4):
            @pl.when((last & 3) == rem)
            def _(rem=rem):
                steps(1 + 4 * n_quads, rem)
                stage_values(last, rem, rem % 2)

    acc = acc_ref[...]
    lane_o = lax.broadcasted_iota(jnp.int32, (tq, 2 * HEAD_DIM), 1)
    o_ref[0] = jnp.where(lane_o < HEAD_DIM, acc[:tq], acc[tq:]).astype(o_ref.dtype)


def _prompt_attn(q, k, v, bias2):
    b, s, _ = q.shape
    tq, tk = ATT_Q_TILE, ATT_K_TILE
    assert tq % tk == 0 and s % tq == 0
    kv_spec = pl.BlockSpec((1, s, 2 * HEAD_DIM), lambda bb, p, i: (bb, 0, p))
    return pl.pallas_call(
        _prompt_attn_kernel,
        grid=(b, N_HEADS // 2, s // tq),
        in_specs=[pl.BlockSpec(memory_space=pltpu.SMEM),
                  pl.BlockSpec((1, tq, 2 * HEAD_DIM), lambda bb, p, i: (bb, i, p)),
                  kv_spec, kv_spec,
                  _const_spec((tk, tk))],
        out_specs=pl.BlockSpec((1, tq, 2 * HEAD_DIM), lambda bb, p, i: (bb, i, p)),
        out_shape=jax.ShapeDtypeStruct((b, s, D_ATT), BF16),
        scratch_shapes=[pltpu.VMEM((2 * tq, 2 * HEAD_DIM), F32),
                        pltpu.VMEM((2 * tq, LANES), F32),
                        pltpu.VMEM((4, 2, 2 * tq, tk), F32),
                        pltpu.VMEM((2, 2, 2 * tq, tk), F32),
                        pltpu.VMEM((2, 2, 2 * tq, LANES), F32)],
        compiler_params=pltpu.CompilerParams(
            dimension_semantics=("arbitrary", "arbitrary", "arbitrary"),
            vmem_limit_bytes=VMEM_LIMIT_BYTES),
        name="prompt_attn",
    )(bias2, q, k, v, _tri(tk))


def _sample_attn_kernel(pt_ref, qbd_ref, knt_ref, vnt_ref, tri_ref, bias_ref, *refs):
    del pt_ref
    npg = SAMPLE_PAGES_PER_STEP
    k_refs, v_refs = refs[:npg], refs[npg:2 * npg]
    o_ref, acc_ref, r_ref = refs[2 * npg:]
    n = pl.program_id(1)
    hq = qbd_ref.shape[1]
    nq = hq // N_HEADS
    qbd = qbd_ref[0]
    bias = bias_ref[...]
    tri = tri_ref[...]

    def step(pages, mask):
        logits_fns = [lambda kt=kt: _dot(qbd, kt().astype(BF16)) + bias for kt, _ in pages]
        value_fns = [lambda w, vt=vt: _dot_nt(w, vt().astype(BF16)) for _, vt in pages]
        _stick_breaking_step(logits_fns, value_fns, tri, [mask] * len(pages), acc_ref, r_ref)

    @pl.when(n == 0)
    def _():
        acc_ref[...] = jnp.zeros_like(acc_ref)
        r_ref[...] = jnp.zeros_like(r_ref)
        kpos = lax.broadcasted_iota(jnp.int32, (hq, PAGE_SIZE), 1)
        qpos = lax.broadcasted_iota(jnp.int32, (hq, PAGE_SIZE), 0) % nq
        step([(lambda: knt_ref[0], lambda: vnt_ref[0])], kpos < qpos)

    def page(ref):
        return lambda: ref[0, 0].reshape(N_HEADS * HEAD_DIM, PAGE_SIZE)

    step([(page(k), page(v)) for k, v in zip(k_refs, v_refs)], None)

    @pl.when(n == pl.num_programs(1) - 1)
    def _():
        acc = acc_ref[...]
        for h in range(N_HEADS):
            o_ref[0, h * nq:(h + 1) * nq, :] = acc[h * nq:(h + 1) * nq,
                                                   h * HEAD_DIM:(h + 1) * HEAD_DIM]


def _sample_attn(page_table, qbd, knt, vnt, cache_kt, cache_vt, bias_col):
    bd, n_pages = page_table.shape
    hq = qbd.shape[1]
    npg = SAMPLE_PAGES_PER_STEP
    assert n_pages % npg == 0

    def page_spec(p):
        return pl.BlockSpec(
            (1, 1, N_HEADS, HEAD_DIM, PAGE_SIZE),
            lambda b, n, pt: (0, pt[b, n_pages - 1 - (n * npg + p)], 0, 0, 0))

    per_req = lambda shape: pl.BlockSpec((1,) + shape, lambda b, n, pt: (b, 0, 0))
    const = lambda shape: pl.BlockSpec(shape, lambda b, n, pt: (0, 0))
    pages = [page_spec(p) for p in range(npg)]
    grid_spec = pltpu.PrefetchScalarGridSpec(
        num_scalar_prefetch=1,
        grid=(bd, n_pages // npg),
        in_specs=[per_req((hq, D_ATT)), per_req((D_ATT, PAGE_SIZE)), per_req((D_ATT, PAGE_SIZE)),
                  const((PAGE_SIZE, PAGE_SIZE)), const((hq, 1))] + pages + pages,
        out_specs=per_req((hq, HEAD_DIM)),
        scratch_shapes=[pltpu.VMEM((hq, D_ATT), F32), pltpu.VMEM((hq, LANES), F32)],
    )
    return pl.pallas_call(
        _sample_attn_kernel,
        grid_spec=grid_spec,
        out_shape=jax.ShapeDtypeStruct((bd, hq, HEAD_DIM), F32),
        compiler_params=pltpu.CompilerParams(
            dimension_semantics=("arbitrary", "arbitrary"), vmem_limit_bytes=VMEM_LIMIT_BYTES),
        name="sample_attn",
    )(page_table, qbd, knt, vnt, _tri(PAGE_SIZE), bias_col,
      *([cache_kt] * npg), *([cache_vt] * npg))


def _output_kernel(x_ref, o_ref, ma_ref, gb_ref, wpa_ref, wout_ref, n2_ref, wup_ref, wdown_ref,
                   nf_ref, y_ref):
    y_att = _dot(o_ref[...], wpa_ref[...])
    merged = ma_ref[...] + gb_ref[...] * y_att
    x1 = x_ref[...] + _dot(merged.astype(BF16), wout_ref[...])
    h2 = _rmsnorm(x1, n2_ref[...]).astype(BF16)
    x2 = x1
    ff_chunk = D_MODEL
    for c in range(0, D_FF, ff_chunk):
        f = jnp.maximum(_dot(h2, wup_ref[:, c:c + ff_chunk]), 0.0)
        x2 = x2 + _dot((f * f).astype(BF16), wdown_ref[c:c + ff_chunk, :])
    y_ref[...] = _rmsnorm(x2, nf_ref[...])


def _output(x, o, ma, gb, wpa, wout, n2, wup, wdown, nf, tm):
    n = x.shape[0]
    row = lambda d: pl.BlockSpec((tm, d), lambda i: (i, 0))
    return pl.pallas_call(
        _output_kernel,
        grid=(n // tm,),
        in_specs=[row(D_MODEL), row(D_ATT), row(D_MODEL), row(D_MODEL),
                  _const_spec((D_ATT, D_MODEL)), _const_spec((D_MODEL, D_MODEL)),
                  _const_spec((1, D_MODEL)), _const_spec((D_MODEL, D_FF)),
                  _const_spec((D_FF, D_MODEL)), _const_spec((1, D_MODEL))],
        out_specs=row(D_MODEL),
        out_shape=jax.ShapeDtypeStruct((n, D_MODEL), F32),
        compiler_params=pltpu.CompilerParams(
            dimension_semantics=("arbitrary",), vmem_limit_bytes=VMEM_LIMIT_BYTES),
        name="output_mlp",
    )(x, o, ma, gb, wpa, wout, n2, wup, wdown, nf)


def kernel(x_prompt, x_sample, cache_k, cache_v, state_conv, page_table, norm1_w, w_in, conv_w,
           w_proj_conv, w_proj_attn, att_bias, w_out, norm2_w, w_up, w_down, norm_f_w):
    depth = w_in.shape[0]
    assert depth == 1, "single-layer stack"
    bp, s, _ = x_prompt.shape
    bd, ls, _ = x_sample.shape

    n1 = norm1_w[0][None, :]
    n2 = norm2_w[0][None, :]
    nf = norm_f_w[None, :]
    win = w_in[0].astype(BF16)
    wpc = w_proj_conv[0].astype(BF16)
    wpa = w_proj_attn[0].astype(BF16)
    wout = w_out[0].astype(BF16)
    wup = w_up[0].astype(BF16)
    wdown = w_down[0].astype(BF16)
    convw = conv_w[0]
    bias2 = att_bias[0] * LOG2E

    xp = x_prompt.reshape(bp * s, D_MODEL)
    qp, kbp, vbp, kp, vp, cup, map_, gbp = _inproj_prompt(xp, n1, win, convw, wpc, s)
    op = _prompt_attn(qp.reshape(bp, s, D_ATT), kbp.reshape(bp, s, D_ATT),
                      vbp.reshape(bp, s, D_ATT), bias2)
    yp = _output(xp, op.reshape(bp * s, D_ATT), map_, gbp, wpa, wout, n2, wup, wdown, nf,
                 TOKEN_TILE)

    xs = x_sample.reshape(bd * ls, D_MODEL)
    st = state_conv[0]
    zeros = jnp.zeros((bd, ls, D_CONV), F32)
    p1 = zeros.at[:, 0].set(st[:, 1]).reshape(bd * ls, D_CONV)
    p2 = zeros.at[:, 0].set(st[:, 0]).at[:, 1].set(st[:, 1]).reshape(bd * ls, D_CONV)
    qs, _, _, ks, vs, cus, mas, gbs = _inproj_sample(xs, n1, win, convw, wpc, p1, p2, ls)
    hq = N_HEADS * ls
    q4 = qs.reshape(bd, ls, N_HEADS, HEAD_DIM).transpose(0, 2, 1, 3)
    eye = jnp.eye(N_HEADS, dtype=BF16)
    qbd = (q4[:, :, :, None, :] * eye[None, :, None, :, None]).reshape(bd, hq, D_ATT)
    bias_col = jnp.repeat(bias2, ls)[:, None]
    pad = ((0, 0), (0, 0), (0, PAGE_SIZE - ls))
    knt = jnp.pad(ks.reshape(bd, ls, D_ATT).transpose(0, 2, 1), pad)
    vnt = jnp.pad(vs.reshape(bd, ls, D_ATT).transpose(0, 2, 1), pad)
    cache_kt = cache_k.transpose(0, 1, 3, 4, 2)
    cache_vt = cache_v.transpose(0, 1, 3, 4, 2)
    os_ = _sample_attn(page_table, qbd, knt, vnt, cache_kt, cache_vt, bias_col)
    os_ = os_.reshape(bd, N_HEADS, ls, HEAD_DIM).transpose(0, 2, 1, 3).reshape(bd * ls, D_ATT)
    ys = _output(xs, os_.astype(BF16), mas, gbs, wpa, wout, n2, wup, wdown, nf, bd * ls)

    def paged(x):
        x = x.reshape(depth, bp, s // PAGE_SIZE, N_HEADS, HEAD_DIM, PAGE_SIZE)
        return x.transpose(0, 1, 2, 5, 3, 4)

    return (
        yp.reshape(bp, s, D_MODEL),
        ys.reshape(bd, ls, D_MODEL),
        paged(kp),
        paged(vp),
        cup.reshape(bp, s, D_CONV)[None, :, s - 2:, :],
        ks.reshape(depth, bd, ls, N_HEADS, HEAD_DIM),
        vs.reshape(depth, bd, ls, N_HEADS, HEAD_DIM),
        cus.reshape(bd, ls, D_CONV)[None, :, ls - 2:, :],
    )
```

```python
import functools
import math

import jax
import jax.numpy as jnp
from jax import lax
from jax.experimental import pallas as pl
from jax.experimental.pallas import tpu as pltpu

D_MODEL = 1024
D_CONV = 512
N_HEADS = 8
HEAD_DIM = 64
D_ATT = N_HEADS * HEAD_DIM
D_FF = 4 * D_MODEL
D_IN = 3 * D_CONV + 3 * D_ATT + 2 * D_MODEL
PAGE_SIZE = 128
LANES = 128
RMS_EPS = 1e-6
LOG2E = math.log2(math.e)

VMEM_LIMIT_BYTES = 56 * 1024 * 1024

TOKEN_TILE = 512
ATT_Q_TILE = 512
ATT_K_TILE = 256
SAMPLE_PAGES_PER_STEP = 32

F32 = jnp.float32
BF16 = jnp.bfloat16


def _dot(a, b):
    return lax.dot_general(a, b, (((1,), (0,)), ((), ())), preferred_element_type=F32)


def _dot_nt(a, b):
    return lax.dot_general(a, b, (((1,), (1,)), ((), ())), preferred_element_type=F32)


def _rmsnorm(x, w):
    return x * lax.rsqrt(jnp.mean(x * x, axis=-1, keepdims=True) + RMS_EPS) * w


def _softplus2(z):
    neg_abs = lax.bitcast_convert_type(
        lax.bitcast_convert_type(z, jnp.uint32) | jnp.uint32(0x80000000), F32)
    return jnp.maximum(z, 0.0) + jnp.log(1.0 + jnp.exp2(neg_abs)) * LOG2E


def _softplus2_operand(z, mask):
    sp = _softplus2(z)
    if mask is not None:
        sp = jnp.where(mask, sp, 0.0)
    return sp


def _weights(z, cs, r, mask):
    r_full = jnp.concatenate([r] * (z.shape[1] // LANES), axis=1) if z.shape[1] > LANES else r
    w = jnp.exp2(z - cs - r_full)
    if mask is not None:
        w = jnp.where(mask, w, 0.0)
    return w


def _row_total(cs):
    return jnp.broadcast_to(cs[:, 0:1], (cs.shape[0], LANES))


def _tri(n):
    j = lax.broadcasted_iota(jnp.int32, (n, n), 0)
    s = lax.broadcasted_iota(jnp.int32, (n, n), 1)
    return (j >= s).astype(BF16)


def _const_spec(shape):
    return pl.BlockSpec(shape, lambda *_: (0,) * len(shape))


def _inproj_body(x_ref, n1_ref, win_ref, convw_ref, wpc_ref, prev_fn,
                 q_ref, kb_ref, vb_ref, k_ref, v_ref, cu_ref, ma_ref, gb_ref):
    x = x_ref[...]
    hb = _rmsnorm(x, n1_ref[...]).astype(BF16)

    def proj(lo, hi):
        return _dot(hb, win_ref[:, lo:hi])

    o = 0
    b_g = proj(o, o + D_CONV); o += D_CONV
    c_g = proj(o, o + D_CONV); o += D_CONV
    u = proj(o, o + D_CONV); o += D_CONV
    cu = c_g * u
    cu_ref[...] = cu
    prev1, prev2 = prev_fn(cu)
    conv = convw_ref[0:1, :] * prev2 + convw_ref[1:2, :] * prev1 + convw_ref[2:3, :] * cu
    y_conv = _dot((b_g * conv).astype(BF16), wpc_ref[...])

    q = proj(o, o + D_ATT); o += D_ATT
    q_ref[...] = (q * (LOG2E * HEAD_DIM ** -0.5)).astype(BF16)
    k = proj(o, o + D_ATT); o += D_ATT
    _store_kv(k_ref, k)
    kb_ref[...] = k.astype(BF16)
    v = proj(o, o + D_ATT); o += D_ATT
    _store_kv(v_ref, v)
    vb_ref[...] = v.astype(BF16)
    gate_conv = proj(o, o + D_MODEL); o += D_MODEL
    ma_ref[...] = jax.nn.sigmoid(gate_conv) * y_conv
    gate_att = proj(o, o + D_MODEL); o += D_MODEL
    gb_ref[...] = jax.nn.sigmoid(gate_att)


def _inproj_prompt_kernel(tiles_per_seq, x_ref, n1_ref, win_ref, convw_ref, wpc_ref,
                          q_ref, kb_ref, vb_ref, k_ref, v_ref, cu_ref, ma_ref, gb_ref, carry_ref):
    tm = x_ref.shape[0]

    @pl.when(pl.program_id(0) % tiles_per_seq == 0)
    def _():
        carry_ref[...] = jnp.zeros_like(carry_ref)

    def prev_fn(cu):
        row = lax.broadcasted_iota(jnp.int32, cu.shape, 0)
        c1 = carry_ref[7:8, :]
        c2 = carry_ref[6:7, :]
        prev1 = jnp.where(row == 0, c1, pltpu.roll(cu, 1, 0))
        prev2 = jnp.where(row == 0, c2, jnp.where(row == 1, c1, pltpu.roll(cu, 2, 0)))
        carry_ref[...] = cu[tm - 8:, :]
        return prev1, prev2

    _inproj_body(x_ref, n1_ref, win_ref, convw_ref, wpc_ref, prev_fn,
                 q_ref, kb_ref, vb_ref, k_ref, v_ref, cu_ref, ma_ref, gb_ref)


def _inproj_sample_kernel(seq, x_ref, n1_ref, win_ref, convw_ref, wpc_ref, p1_ref, p2_ref,
                          q_ref, kb_ref, vb_ref, k_ref, v_ref, cu_ref, ma_ref, gb_ref):
    def prev_fn(cu):
        pos = lax.broadcasted_iota(jnp.int32, cu.shape, 0) % seq
        prev1 = jnp.where(pos == 0, p1_ref[...], pltpu.roll(cu, 1, 0))
        prev2 = jnp.where(pos < 2, p2_ref[...], pltpu.roll(cu, 2, 0))
        return prev1, prev2

    _inproj_body(x_ref, n1_ref, win_ref, convw_ref, wpc_ref, prev_fn,
                 q_ref, kb_ref, vb_ref, k_ref, v_ref, cu_ref, ma_ref, gb_ref)


def _store_kv(ref, x):
    if len(ref.shape) == 2:
        ref[...] = x
    else:
        for p in range(ref.shape[0]):
            ref[p] = x[p * PAGE_SIZE:(p + 1) * PAGE_SIZE, :].T


def _inproj_out(n, tm, paged_kv):
    row = lambda d: pl.BlockSpec((tm, d), lambda i: (i, 0))
    if paged_kv:
        kv_shape = jax.ShapeDtypeStruct((n // PAGE_SIZE, D_ATT, PAGE_SIZE), F32)
        kv_spec = pl.BlockSpec((tm // PAGE_SIZE, D_ATT, PAGE_SIZE), lambda i: (i, 0, 0))
    else:
        kv_shape = jax.ShapeDtypeStruct((n, D_ATT), F32)
        kv_spec = row(D_ATT)
    shapes = [
        jax.ShapeDtypeStruct((n, D_ATT), BF16),
        jax.ShapeDtypeStruct((n, D_ATT), BF16),
        jax.ShapeDtypeStruct((n, D_ATT), BF16),
        kv_shape,
        kv_shape,
        jax.ShapeDtypeStruct((n, D_CONV), F32),
        jax.ShapeDtypeStruct((n, D_MODEL), F32),
        jax.ShapeDtypeStruct((n, D_MODEL), F32),
    ]
    specs = [row(D_ATT)] * 3 + [kv_spec] * 2 + [row(D_CONV), row(D_MODEL), row(D_MODEL)]
    return shapes, specs


def _inproj_weight_specs():
    return [_const_spec((1, D_MODEL)), _const_spec((D_MODEL, D_IN)),
            _const_spec((3, D_CONV)), _const_spec((D_CONV, D_MODEL))]


def _inproj_prompt(x, n1, win, convw, wpc, seq_len):
    n = x.shape[0]
    tm = TOKEN_TILE
    shapes, specs = _inproj_out(n, tm, paged_kv=True)
    return pl.pallas_call(
        functools.partial(_inproj_prompt_kernel, seq_len // tm),
        grid=(n // tm,),
        in_specs=[pl.BlockSpec((tm, D_MODEL), lambda i: (i, 0))] + _inproj_weight_specs(),
        out_specs=specs,
        out_shape=shapes,
        scratch_shapes=[pltpu.VMEM((8, D_CONV), F32)],
        compiler_params=pltpu.CompilerParams(
            dimension_semantics=("arbitrary",), vmem_limit_bytes=VMEM_LIMIT_BYTES),
        name="inproj_prompt",
    )(x, n1, win, convw, wpc)


def _inproj_sample(x, n1, win, convw, wpc, p1, p2, seq):
    n = x.shape[0]
    shapes, specs = _inproj_out(n, n, paged_kv=False)
    full = lambda d: pl.BlockSpec((n, d), lambda i: (0, 0))
    return pl.pallas_call(
        functools.partial(_inproj_sample_kernel, seq),
        grid=(1,),
        in_specs=[full(D_MODEL)] + _inproj_weight_specs() + [full(D_CONV), full(D_CONV)],
        out_specs=specs,
        out_shape=shapes,
        compiler_params=pltpu.CompilerParams(
            dimension_semantics=("arbitrary",), vmem_limit_bytes=VMEM_LIMIT_BYTES),
        name="inproj_sample",
    )(x, n1, win, convw, wpc, p1, p2)


def _stick_breaking_step(logits_fns, value_fns, tri, masks, acc_ref, r_ref):
    zs = [f() for f in logits_fns]
    sums = [_dot(_softplus2_operand(z, m), tri) for z, m in zip(zs, masks)]
    r = r_ref[...]
    acc = acc_ref[...]
    for z, cs, m, value_fn in zip(zs, sums, masks, value_fns):
        acc = acc + value_fn(_weights(z, cs, r, m))
        r = r + _row_total(cs)
    acc_ref[...] = acc
    r_ref[...] = r


def _prompt_attn_kernel(bias_ref, q_ref, k_ref, v_ref, tri_ref, o_ref,
                        acc_ref, r_ref, z_buf, c_buf, t_buf):
    tq, tk = ATT_Q_TILE, ATT_K_TILE
    pair = pl.program_id(1)
    i = pl.program_id(2)

    qq = q_ref[0]
    lane = lax.broadcasted_iota(jnp.int32, qq.shape, 1)
    zero = jnp.zeros_like(qq)
    qs = jnp.concatenate([jnp.where(lane < HEAD_DIM, qq, zero),
                          jnp.where(lane >= HEAD_DIM, qq, zero)], axis=0)
    row = lax.broadcasted_iota(jnp.int32, (2 * tq, 1), 0)
    bias = jnp.where(row < tq, bias_ref[2 * pair], bias_ref[2 * pair + 1])

    acc_ref[...] = jnp.zeros_like(acc_ref)
    r_ref[...] = jnp.zeros_like(r_ref)
    tri = tri_ref[...]

    def keys(j):
        return k_ref[0, pl.ds(pl.multiple_of(j * tk, tk), tk), :]

    def values(j):
        return v_ref[0, pl.ds(pl.multiple_of(j * tk, tk), tk), :]

    def logits(j):
        return _dot_nt(qs, keys(j)) + bias

    n_diag = tq // tk
    first_diag = i * n_diag
    odd = first_diag & 1
    n_pairs = lax.shift_right_logical(first_diag, 1)
    top = first_diag - 1 - odd

    def pair(m):
        return [top - 2 * m - sub for sub in range(2)]

    def stage_logits(m, zslot):
        for sub, j in enumerate(pair(m)):
            z_buf[zslot, sub] = logits(jnp.maximum(j, 0))

    def stage_sums(zslot, cslot):
        for sub in range(2):
            cs = _dot(_softplus2_operand(z_buf[zslot, sub], None), tri)
            c_buf[cslot, sub] = cs
            t_buf[cslot, sub] = _row_total(cs)

    def stage_values(m, zslot, cslot):
        r = r_ref[...]
        acc = acc_ref[...]
        for sub, j in enumerate(pair(m)):
            w = _weights(z_buf[zslot, sub], c_buf[cslot, sub], r, None)
            acc = acc + _dot(w, values(j))
            r = r + t_buf[cslot, sub]
        acc_ref[...] = acc
        r_ref[...] = r

    def step(t, phase):
        za, zb, zc = (phase + 1) % 4, phase, (phase - 1) % 4
        cb, cc = phase % 2, (phase - 1) % 2
        r = r_ref[...]
        acc = acc_ref[...]
        for sub in range(2):
            w = _weights(z_buf[zc, sub], c_buf[cc, sub], r, None)
            acc = acc + _dot(w, values(pair(t - 1)[sub]))
            r = r + t_buf[cc, sub]
            z_buf[za, sub] = logits(jnp.maximum(pair(t + 1)[sub], 0))
            cs = _dot(_softplus2_operand(z_buf[zb, sub], None), tri)
            c_buf[cb, sub] = cs
            t_buf[cb, sub] = _row_total(cs)
        acc_ref[...] = acc
        r_ref[...] = r

    def head_step(js, masks):
        stage_logits(0, 0)
        stage_logits(1, 1)
        _stick_breaking_step([functools.partial(logits, j) for j in js],
                             [lambda w, j=j: _dot(w, values(j)) for j in js],
                             tri, masks, acc_ref, r_ref)
        stage_sums(0, 0)

    qpos = lax.broadcasted_iota(jnp.int32, (2 * tq, tk), 0) % tq
    kpos = lax.broadcasted_iota(jnp.int32, (2 * tq, tk), 1)
    diag = [first_diag + d for d in reversed(range(n_diag))]
    causal = [kpos + d * tk < qpos for d in reversed(range(n_diag))]

    if n_diag % 2 == 0:
        head_step(diag, causal)
    else:
        @pl.when(odd == 0)
        def _():
            head_step(diag, causal)

        @pl.when(odd == 1)
        def _():
            head_step(diag + [first_diag - 1], causal + [None])

    @pl.when(n_pairs >= 1)
    def _():
        last = n_pairs - 1
        n_quads = lax.shift_right_logical(last, 2)

        def steps(t, count):
            for d in range(count):
                step(t + d, (1 + d) % 4)

        def body(u, carry):
            steps(1 + 4 * u, 4)
            return carry

        lax.fori_loop(0, n_quads, body, 0)
        for rem in range(4):
            @pl.when((last & 3) == rem)
            def _(rem=rem):
                steps(1 + 4 * n_quads, rem)
                stage_values(last, rem, rem % 2)

    acc = acc_ref[...]
    lane_o = lax.broadcasted_iota(jnp.int32, (tq, 2 * HEAD_DIM), 1)
    o_ref[0] = jnp.where(lane_o < HEAD_DIM, acc[:tq], acc[tq:]).astype(o_ref.dtype)


def _prompt_attn(q, k, v, bias2):
    b, s, _ = q.shape
    tq, tk = ATT_Q_TILE, ATT_K_TILE
    assert tq % tk == 0 and s % tq == 0
    kv_spec = pl.BlockSpec((1, s, 2 * HEAD_DIM), lambda bb, p, i: (bb, 0, p))
    return pl.pallas_call(
        _prompt_attn_kernel,
        grid=(b, N_HEADS // 2, s // tq),
        in_specs=[pl.BlockSpec(memory_space=pltpu.SMEM),
                  pl.BlockSpec((1, tq, 2 * HEAD_DIM), lambda bb, p, i: (bb, i, p)),
                  kv_spec, kv_spec,
                  _const_spec((tk, tk))],
        out_specs=pl.BlockSpec((1, tq, 2 * HEAD_DIM), lambda bb, p, i: (bb, i, p)),
        out_shape=jax.ShapeDtypeStruct((b, s, D_ATT), BF16),
        scratch_shapes=[pltpu.VMEM((2 * tq, 2 * HEAD_DIM), F32),
                        pltpu.VMEM((2 * tq, LANES), F32),
                        pltpu.VMEM((4, 2, 2 * tq, tk), F32),
                        pltpu.VMEM((2, 2, 2 * tq, tk), F32),
                        pltpu.VMEM((2, 2, 2 * tq, LANES), F32)],
        compiler_params=pltpu.CompilerParams(
            dimension_semantics=("arbitrary", "arbitrary", "arbitrary"),
            vmem_limit_bytes=VMEM_LIMIT_BYTES),
        name="prompt_attn",
    )(bias2, q, k, v, _tri(tk))


def _sample_attn_kernel(pt_ref, qbd_ref, knt_ref, vnt_ref, tri_ref, bias_ref, *refs):
    del pt_ref
    npg = SAMPLE_PAGES_PER_STEP
    k_refs, v_refs = refs[:npg], refs[npg:2 * npg]
    o_ref, acc_ref, r_ref = refs[2 * npg:]
    n = pl.program_id(1)
    hq = qbd_ref.shape[1]
    nq = hq // N_HEADS
    qbd = qbd_ref[0]
    bias = bias_ref[...]
    tri = tri_ref[...]

    def step(pages, mask):
        logits_fns = [lambda kt=kt: _dot(qbd, kt()) + bias for kt, _ in pages]
        value_fns = [lambda w, vt=vt: _dot_nt(w, vt()) for _, vt in pages]
        _stick_breaking_step(logits_fns, value_fns, tri, [mask] * len(pages), acc_ref, r_ref)

    @pl.when(n == 0)
    def _():
        acc_ref[...] = jnp.zeros_like(acc_ref)
        r_ref[...] = jnp.zeros_like(r_ref)
        kpos = lax.broadcasted_iota(jnp.int32, (hq, PAGE_SIZE), 1)
        qpos = lax.broadcasted_iota(jnp.int32, (hq, PAGE_SIZE), 0) % nq
        step([(lambda: knt_ref[0], lambda: vnt_ref[0])], kpos < qpos)

    def page(ref):
        return lambda: ref[0, 0].reshape(N_HEADS * HEAD_DIM, PAGE_SIZE)

    step([(page(k), page(v)) for k, v in zip(k_refs, v_refs)], None)

    @pl.when(n == pl.num_programs(1) - 1)
    def _():
        acc = acc_ref[...]
        for h in range(N_HEADS):
            o_ref[0, h * nq:(h + 1) * nq, :] = acc[h * nq:(h + 1) * nq,
                                                   h * HEAD_DIM:(h + 1) * HEAD_DIM]


def _sample_attn(page_table, qbd, knt, vnt, cache_kt, cache_vt, bias_col):
    bd, n_pages = page_table.shape
    hq = qbd.shape[1]
    npg = SAMPLE_PAGES_PER_STEP
    assert n_pages % npg == 0

    def page_spec(p):
        return pl.BlockSpec(
            (1, 1, N_HEADS, HEAD_DIM, PAGE_SIZE),
            lambda b, n, pt: (0, pt[b, n_pages - 1 - (n * npg + p)], 0, 0, 0))

    per_req = lambda shape: pl.BlockSpec((1,) + shape, lambda b, n, pt: (b, 0, 0))
    const = lambda shape: pl.BlockSpec(shape, lambda b, n, pt: (0, 0))
    pages = [page_spec(p) for p in range(npg)]
    grid_spec = pltpu.PrefetchScalarGridSpec(
        num_scalar_prefetch=1,
        grid=(bd, n_pages // npg),
        in_specs=[per_req((hq, D_ATT)), per_req((D_ATT, PAGE_SIZE)), per_req((D_ATT, PAGE_SIZE)),
                  const((PAGE_SIZE, PAGE_SIZE)), const((hq, 1))] + pages + pages,
        out_specs=per_req((hq, HEAD_DIM)),
        scratch_shapes=[pltpu.VMEM((hq, D_ATT), F32), pltpu.VMEM((hq, LANES), F32)],
    )
    return pl.pallas_call(
        _sample_attn_kernel,
        grid_spec=grid_spec,
        out_shape=jax.ShapeDtypeStruct((bd, hq, HEAD_DIM), F32),
        compiler_params=pltpu.CompilerParams(
            dimension_semantics=("arbitrary", "arbitrary"), vmem_limit_bytes=VMEM_LIMIT_BYTES),
        name="sample_attn",
    )(page_table, qbd, knt, vnt, _tri(PAGE_SIZE), bias_col,
      *([cache_kt] * npg), *([cache_vt] * npg))


def _output_kernel(x_ref, o_ref, ma_ref, gb_ref, wpa_ref, wout_ref, n2_ref, wup_ref, wdown_ref,
                   nf_ref, y_ref):
    y_att = _dot(o_ref[...], wpa_ref[...])
    merged = ma_ref[...] + gb_ref[...] * y_att
    x1 = x_ref[...] + _dot(merged.astype(BF16), wout_ref[...])
    h2 = _rmsnorm(x1, n2_ref[...]).astype(BF16)
    x2 = x1
    ff_chunk = D_MODEL
    for c in range(0, D_FF, ff_chunk):
        f = jnp.maximum(_dot(h2, wup_ref[:, c:c + ff_chunk]), 0.0)
        x2 = x2 + _dot((f * f).astype(BF16), wdown_ref[c:c + ff_chunk, :])
    y_ref[...] = _rmsnorm(x2, nf_ref[...])


def _output(x, o, ma, gb, wpa, wout, n2, wup, wdown, nf, tm):
    n = x.shape[0]
    row = lambda d: pl.BlockSpec((tm, d), lambda i: (i, 0))
    return pl.pallas_call(
        _output_kernel,
        grid=(n // tm,),
        in_specs=[row(D_MODEL), row(D_ATT), row(D_MODEL), row(D_MODEL),
                  _const_spec((D_ATT, D_MODEL)), _const_spec((D_MODEL, D_MODEL)),
                  _const_spec((1, D_MODEL)), _const_spec((D_MODEL, D_FF)),
                  _const_spec((D_FF, D_MODEL)), _const_spec((1, D_MODEL))],
        out_specs=row(D_MODEL),
        out_shape=jax.ShapeDtypeStruct((n, D_MODEL), F32),
        compiler_params=pltpu.CompilerParams(
            dimension_semantics=("arbitrary",), vmem_limit_bytes=VMEM_LIMIT_BYTES),
        name="output_mlp",
    )(x, o, ma, gb, wpa, wout, n2, wup, wdown, nf)


def kernel(x_prompt, x_sample, cache_k, cache_v, state_conv, page_table, norm1_w, w_in, conv_w,
           w_proj_conv, w_proj_attn, att_bias, w_out, norm2_w, w_up, w_down, norm_f_w):
    depth = w_in.shape[0]
    assert depth == 1, "single-layer stack"
    bp, s, _ = x_prompt.shape
    bd, ls, _ = x_sample.shape

    n1 = norm1_w[0][None, :]
    n2 = norm2_w[0][None, :]
    nf = norm_f_w[None, :]
    win = w_in[0].astype(BF16)
    wpc = w_proj_conv[0].astype(BF16)
    wpa = w_proj_attn[0].astype(BF16)
    wout = w_out[0].astype(BF16)
    wup = w_up[0].astype(BF16)
    wdown = w_down[0].astype(BF16)
    convw = conv_w[0]
    bias2 = att_bias[0] * LOG2E

    xp = x_prompt.reshape(bp * s, D_MODEL)
    qp, kbp, vbp, kp, vp, cup, map_, gbp = _inproj_prompt(xp, n1, win, convw, wpc, s)
    op = _prompt_attn(qp.reshape(bp, s, D_ATT), kbp.reshape(bp, s, D_ATT),
                      vbp.reshape(bp, s, D_ATT), bias2)
    yp = _output(xp, op.reshape(bp * s, D_ATT), map_, gbp, wpa, wout, n2, wup, wdown, nf,
                 TOKEN_TILE)

    xs = x_sample.reshape(bd * ls, D_MODEL)
    st = state_conv[0]
    zeros = jnp.zeros((bd, ls, D_CONV), F32)
    p1 = zeros.at[:, 0].set(st[:, 1]).reshape(bd * ls, D_CONV)
    p2 = zeros.at[:, 0].set(st[:, 0]).at[:, 1].set(st[:, 1]).reshape(bd * ls, D_CONV)
    qs, _, _, ks, vs, cus, mas, gbs = _inproj_sample(xs, n1, win, convw, wpc, p1, p2, ls)
    hq = N_HEADS * ls
    q4 = qs.reshape(bd, ls, N_HEADS, HEAD_DIM).transpose(0, 2, 1, 3)
    eye = jnp.eye(N_HEADS, dtype=BF16)
    qbd = (q4[:, :, :, None, :] * eye[None, :, None, :, None]).reshape(bd, hq, D_ATT)
    bias_col = jnp.repeat(bias2, ls)[:, None]
    pad = ((0, 0), (0, 0), (0, PAGE_SIZE - ls))
    knt = jnp.pad(ks.reshape(bd, ls, D_ATT).transpose(0, 2, 1), pad)
    vnt = jnp.pad(vs.reshape(bd, ls, D_ATT).transpose(0, 2, 1), pad)
    cache_kt = cache_k.transpose(0, 1, 3, 4, 2)
    cache_vt = cache_v.transpose(0, 1, 3, 4, 2)
    os_ = _sample_attn(page_table, qbd, knt, vnt, cache_kt, cache_vt, bias_col)
    os_ = os_.reshape(bd, N_HEADS, ls, HEAD_DIM).transpose(0, 2, 1, 3).reshape(bd * ls, D_ATT)
    ys = _output(xs, os_.astype(BF16), mas, gbs, wpa, wout, n2, wup, wdown, nf, bd * ls)

    def paged(x):
        x = x.reshape(depth, bp, s // PAGE_SIZE, N_HEADS, HEAD_DIM, PAGE_SIZE)
        return x.transpose(0, 1, 2, 5, 3, 4)

    return (
        yp.reshape(bp, s, D_MODEL),
        ys.reshape(bd, ls, D_MODEL),
        paged(kp),
        paged(vp),
        cup.reshape(bp, s, D_CONV)[None, :, s - 2:, :],
        ks.reshape(depth, bd, ls, N_HEADS, HEAD_DIM),
        vs.reshape(depth, bd, ls, N_HEADS, HEAD_DIM),
        cus.reshape(bd, ls, D_CONV)[None, :, ls - 2:, :],
    )
```

```python
import functools
import math

import jax
import jax.numpy as jnp
from jax import lax
from jax.experimental import pallas as pl
from jax.experimental.pallas import tpu as pltpu

D_MODEL = 1024
D_CONV = 512
N_HEADS = 8
HEAD_DIM = 64
D_ATT = N_HEADS * HEAD_DIM
D_FF = 4 * D_MODEL
D_IN = 3 * D_CONV + 3 * D_ATT + 2 * D_MODEL
PAGE_SIZE = 128
LANES = 128
RMS_EPS = 1e-6
LOG2E = math.log2(math.e)

VMEM_LIMIT_BYTES = 56 * 1024 * 1024

TOKEN_TILE = 512
ATT_Q_TILE = 512
ATT_K_TILE = 256
SAMPLE_PAGES_PER_STEP = 32

F32 = jnp.float32
BF16 = jnp.bfloat16


def _dot(a, b):
    return lax.dot_general(a, b, (((1,), (0,)), ((), ())), preferred_element_type=F32)


def _dot_nt(a, b):
    return lax.dot_general(a, b, (((1,), (1,)), ((), ())), preferred_element_type=F32)


def _rmsnorm(x, w):
    return x * lax.rsqrt(jnp.mean(x * x, axis=-1, keepdims=True) + RMS_EPS) * w


def _softplus2(z):
    neg_abs = lax.bitcast_convert_type(
        lax.bitcast_convert_type(z, jnp.uint32) | jnp.uint32(0x80000000), F32)
    return jnp.maximum(z, 0.0) + jnp.log(1.0 + jnp.exp2(neg_abs)) * LOG2E


def _softplus2_operand(z, mask):
    sp = _softplus2(z)
    if mask is not None:
        sp = jnp.where(mask, sp, 0.0)
    return sp


def _weights(z, cs, r, mask):
    r_full = jnp.concatenate([r] * (z.shape[1] // LANES), axis=1) if z.shape[1] > LANES else r
    w = jnp.exp2(z - cs - r_full)
    if mask is not None:
        w = jnp.where(mask, w, 0.0)
    return w


def _row_total(cs):
    return jnp.broadcast_to(cs[:, 0:1], (cs.shape[0], LANES))


def _tri(n):
    j = lax.broadcasted_iota(jnp.int32, (n, n), 0)
    s = lax.broadcasted_iota(jnp.int32, (n, n), 1)
    return (j >= s).astype(BF16)


def _const_spec(shape):
    return pl.BlockSpec(shape, lambda *_: (0,) * len(shape))


def _inproj_body(x_ref, n1_ref, win_ref, convw_ref, wpc_ref, prev_fn,
                 q_ref, kb_ref, vb_ref, k_ref, v_ref, cu_ref, ma_ref, gb_ref):
    x = x_ref[...]
    hb = _rmsnorm(x, n1_ref[...]).astype(BF16)

    def proj(lo, hi):
        return _dot(hb, win_ref[:, lo:hi])

    o = 0
    b_g = proj(o, o + D_CONV); o += D_CONV
    c_g = proj(o, o + D_CONV); o += D_CONV
    u = proj(o, o + D_CONV); o += D_CONV
    cu = c_g * u
    cu_ref[...] = cu
    prev1, prev2 = prev_fn(cu)
    conv = convw_ref[0:1, :] * prev2 + convw_ref[1:2, :] * prev1 + convw_ref[2:3, :] * cu
    y_conv = _dot((b_g * conv).astype(BF16), wpc_ref[...])

    q = proj(o, o + D_ATT); o += D_ATT
    q_ref[...] = (q * (LOG2E * HEAD_DIM ** -0.5)).astype(BF16)
    k = proj(o, o + D_ATT); o += D_ATT
    _store_kv(k_ref, k)
    kb_ref[...] = k.astype(BF16)
    v = proj(o, o + D_ATT); o += D_ATT
    _store_kv(v_ref, v)
    vb_ref[...] = v.astype(BF16)
    gate_conv = proj(o, o + D_MODEL); o += D_MODEL
    ma_ref[...] = jax.nn.sigmoid(gate_conv) * y_conv
    gate_att = proj(o, o + D_MODEL); o += D_MODEL
    gb_ref[...] = jax.nn.sigmoid(gate_att)


def _inproj_prompt_kernel(tiles_per_seq, x_ref, n1_ref, win_ref, convw_ref, wpc_ref,
                          q_ref, kb_ref, vb_ref, k_ref, v_ref, cu_ref, ma_ref, gb_ref, carry_ref):
    tm = x_ref.shape[0]

    @pl.when(pl.program_id(0) % tiles_per_seq == 0)
    def _():
        carry_ref[...] = jnp.zeros_like(carry_ref)

    def prev_fn(cu):
        row = lax.broadcasted_iota(jnp.int32, cu.shape, 0)
        c1 = carry_ref[7:8, :]
        c2 = carry_ref[6:7, :]
        prev1 = jnp.where(row == 0, c1, pltpu.roll(cu, 1, 0))
        prev2 = jnp.where(row == 0, c2, jnp.where(row == 1, c1, pltpu.roll(cu, 2, 0)))
        carry_ref[...] = cu[tm - 8:, :]
        return prev1, prev2

    _inproj_body(x_ref, n1_ref, win_ref, convw_ref, wpc_ref, prev_fn,
                 q_ref, kb_ref, vb_ref, k_ref, v_ref, cu_ref, ma_ref, gb_ref)


def _inproj_sample_kernel(seq, x_ref, n1_ref, win_ref, convw_ref, wpc_ref, p1_ref, p2_ref,
                          q_ref, kb_ref, vb_ref, k_ref, v_ref, cu_ref, ma_ref, gb_ref):
    def prev_fn(cu):
        pos = lax.broadcasted_iota(jnp.int32, cu.shape, 0) % seq
        prev1 = jnp.where(pos == 0, p1_ref[...], pltpu.roll(cu, 1, 0))
        prev2 = jnp.where(pos < 2, p2_ref[...], pltpu.roll(cu, 2, 0))
        return prev1, prev2

    _inproj_body(x_ref, n1_ref, win_ref, convw_ref, wpc_ref, prev_fn,
                 q_ref, kb_ref, vb_ref, k_ref, v_ref, cu_ref, ma_ref, gb_ref)


def _store_kv(ref, x):
    if len(ref.shape) == 2:
        ref[...] = x
    else:
        for p in range(ref.shape[0]):
            ref[p] = x[p * PAGE_SIZE:(p + 1) * PAGE_SIZE, :].T


def _inproj_out(n, tm, paged_kv):
    row = lambda d: pl.BlockSpec((tm, d), lambda i: (i, 0))
    if paged_kv:
        kv_shape = jax.ShapeDtypeStruct((n // PAGE_SIZE, D_ATT, PAGE_SIZE), F32)
        kv_spec = pl.BlockSpec((tm // PAGE_SIZE, D_ATT, PAGE_SIZE), lambda i: (i, 0, 0))
    else:
        kv_shape = jax.ShapeDtypeStruct((n, D_ATT), F32)
        kv_spec = row(D_ATT)
    shapes = [
        jax.ShapeDtypeStruct((n, D_ATT), BF16),
        jax.ShapeDtypeStruct((n, D_ATT), BF16),
        jax.ShapeDtypeStruct((n, D_ATT), BF16),
        kv_shape,
        kv_shape,
        jax.ShapeDtypeStruct((n, D_CONV), F32),
        jax.ShapeDtypeStruct((n, D_MODEL), F32),
        jax.ShapeDtypeStruct((n, D_MODEL), F32),
    ]
    specs = [row(D_ATT)] * 3 + [kv_spec] * 2 + [row(D_CONV), row(D_MODEL), row(D_MODEL)]
    return shapes, specs


def _inproj_weight_specs():
    return [_const_spec((1, D_MODEL)), _const_spec((D_MODEL, D_IN)),
            _const_spec((3, D_CONV)), _const_spec((D_CONV, D_MODEL))]


def _inproj_prompt(x, n1, win, convw, wpc, seq_len):
    n = x.shape[0]
    tm = TOKEN_TILE
    shapes, specs = _inproj_out(n, tm, paged_kv=True)
    return pl.pallas_call(
        functools.partial(_inproj_prompt_kernel, seq_len // tm),
        grid=(n // tm,),
        in_specs=[pl.BlockSpec((tm, D_MODEL), lambda i: (i, 0))] + _inproj_weight_specs(),
        out_specs=specs,
        out_shape=shapes,
        scratch_shapes=[pltpu.VMEM((8, D_CONV), F32)],
        compiler_params=pltpu.CompilerParams(
            dimension_semantics=("arbitrary",), vmem_limit_bytes=VMEM_LIMIT_BYTES),
        name="inproj_prompt",
    )(x, n1, win, convw, wpc)


def _inproj_sample(x, n1, win, convw, wpc, p1, p2, seq):
    n = x.shape[0]
    shapes, specs = _inproj_out(n, n, paged_kv=False)
    full = lambda d: pl.BlockSpec((n, d), lambda i: (0, 0))
    return pl.pallas_call(
        functools.partial(_inproj_sample_kernel, seq),
        grid=(1,),
        in_specs=[full(D_MODEL)] + _inproj_weight_specs() + [full(D_CONV), full(D_CONV)],
        out_specs=specs,
        out_shape=shapes,
        compiler_params=pltpu.CompilerParams(
            dimension_semantics=("arbitrary",), vmem_limit_bytes=VMEM_LIMIT_BYTES),
        name="inproj_sample",
    )(x, n1, win, convw, wpc, p1, p2)


def _stick_breaking_step(logits_fns, value_fns, tri, masks, acc_ref, r_ref):
    zs = [f() for f in logits_fns]
    sums = [_dot(_softplus2_operand(z, m), tri) for z, m in zip(zs, masks)]
    r = r_ref[...]
    acc = acc_ref[...]
    for z, cs, m, value_fn in zip(zs, sums, masks, value_fns):
        acc = acc + value_fn(_weights(z, cs, r, m))
        r = r + _row_total(cs)
    acc_ref[...] = acc
    r_ref[...] = r


def _prompt_attn_kernel(bias_ref, q_ref, k_ref, v_ref, tri_ref, o_ref,
                        acc_ref, r_ref, z_buf, c_buf, t_buf):
    tq, tk = ATT_Q_TILE, ATT_K_TILE
    pair = pl.program_id(1)
    i = pl.program_id(2)

    assert tq == 2 * tk
    qq = q_ref[0]
    lane = lax.broadcasted_iota(jnp.int32, qq.shape, 1)
    zero = jnp.zeros_like(qq)
    qa = jnp.where(lane < HEAD_DIM, qq, zero)
    qb = jnp.where(lane >= HEAD_DIM, qq, zero)
    qs = jnp.concatenate([qa[:tk], qb[:tk], qa[tk:], qb[tk:]], axis=0)
    row = lax.broadcasted_iota(jnp.int32, (2 * tq, 1), 0)
    bias = jnp.where((row // tk) % 2 == 0, bias_ref[2 * pair], bias_ref[2 * pair + 1])
    tri = tri_ref[...]

    def keys(j):
        return k_ref[0, pl.ds(pl.multiple_of(j * tk, tk), tk), :]

    def values(j):
        return v_ref[0, pl.ds(pl.multiple_of(j * tk, tk), tk), :]

    def logits(j):
        return _dot_nt(qs, keys(j)) + bias

    first_diag = 2 * i
    n_pairs = i
    top = first_diag - 1

    def pair(m):
        return [top - 2 * m - sub for sub in range(2)]

    def stage_logits(m, zslot):
        for sub, j in enumerate(pair(m)):
            z_buf[zslot, sub] = logits(jnp.maximum(j, 0))

    def stage_sums(zslot, cslot):
        for sub in range(2):
            cs = _dot(_softplus2_operand(z_buf[zslot, sub], None), tri)
            c_buf[cslot, sub] = cs
            t_buf[cslot, sub] = _row_total(cs)

    def stage_values(m, zslot, cslot):
        r = r_ref[...]
        acc = acc_ref[...]
        for sub, j in enumerate(pair(m)):
            w = _weights(z_buf[zslot, sub], c_buf[cslot, sub], r, None)
            acc = acc + _dot(w, values(j))
            r = r + t_buf[cslot, sub]
        acc_ref[...] = acc
        r_ref[...] = r

    def step(t, phase, more_pairs=True):
        za, zb, zc = (phase + 1) % 4, phase, (phase - 1) % 4
        cb, cc = phase % 2, (phase - 1) % 2
        r = r_ref[...]
        acc = acc_ref[...]
        for sub in range(2):
            w = _weights(z_buf[zc, sub], c_buf[cc, sub], r, None)
            acc = acc + _dot(w, values(pair(t - 1)[sub]))
            r = r + t_buf[cc, sub]
            if more_pairs:
                z_buf[za, sub] = logits(jnp.maximum(pair(t + 1)[sub], 0))
            cs = _dot(_softplus2_operand(z_buf[zb, sub], None), tri)
            c_buf[cb, sub] = cs
            t_buf[cb, sub] = _row_total(cs)
        acc_ref[...] = acc
        r_ref[...] = r

    triangle = (lax.broadcasted_iota(jnp.int32, (tq, tk), 1)
                < lax.broadcasted_iota(jnp.int32, (tq, tk), 0) % tk)
    z_hi = _dot_nt(qs[tq:], keys(first_diag + 1)) + bias[tq:]
    z_lo = logits(first_diag)
    stage_logits(0, 0)
    cs_hi = _dot(_softplus2_operand(z_hi, triangle), tri)
    sp_lo = jnp.concatenate([_softplus2_operand(z_lo[:tq], triangle),
                             _softplus2_operand(z_lo[tq:], None)], axis=0)
    cs_lo = _dot(sp_lo, tri)
    stage_logits(1, 1)
    no_total = jnp.zeros((tq, LANES), F32)
    pv_hi = _dot(_weights(z_hi, cs_hi, no_total, triangle), values(first_diag + 1))
    r_hi = jnp.concatenate([no_total, _row_total(cs_hi)], axis=0)
    w_lo = jnp.concatenate([_weights(z_lo[:tq], cs_lo[:tq], r_hi[:tq], triangle),
                            _weights(z_lo[tq:], cs_lo[tq:], r_hi[tq:], None)], axis=0)
    acc_ref[...] = (_dot(w_lo, values(first_diag))
                    + jnp.concatenate([jnp.zeros_like(pv_hi), pv_hi], axis=0))
    r_ref[...] = r_hi + _row_total(cs_lo)
    stage_sums(0, 0)

    @pl.when(n_pairs >= 1)
    def _():
        last = n_pairs - 1
        n_quads = lax.shift_right_logical(last, 2)

        def steps(t, count, ends_at_last=False):
            for d in range(count):
                step(t + d, (1 + d) % 4, more_pairs=not (ends_at_last and d == count - 1))

        def body(u, carry):
            steps(1 + 4 * u, 4)
            return carry

        lax.fori_loop(0, n_quads, body, 0)
        for rem in range(4):
            @pl.when((last & 3) == rem)
            def _(rem=rem):
                steps(1 + 4 * n_quads, rem, ends_at_last=True)
                stage_values(last, rem, rem % 2)

    acc = acc_ref[...]
    first_head = lax.broadcasted_iota(jnp.int32, (tk, 2 * HEAD_DIM), 1) < HEAD_DIM
    for half in range(2):
        rows = acc[2 * half * tk:2 * (half + 1) * tk]
        o_ref[0, half * tk:(half + 1) * tk, :] = jnp.where(
            first_head, rows[:tk], rows[tk:]).astype(o_ref.dtype)


def _prompt_attn(q, k, v, bias2):
    b, s, _ = q.shape
    tq, tk = ATT_Q_TILE, ATT_K_TILE
    assert tq % tk == 0 and s % tq == 0
    kv_spec = pl.BlockSpec((1, s, 2 * HEAD_DIM), lambda bb, p, i: (bb, 0, p))
    return pl.pallas_call(
        _prompt_attn_kernel,
        grid=(b, N_HEADS // 2, s // tq),
        in_specs=[pl.BlockSpec(memory_space=pltpu.SMEM),
                  pl.BlockSpec((1, tq, 2 * HEAD_DIM), lambda bb, p, i: (bb, i, p)),
                  kv_spec, kv_spec,
                  _const_spec((tk, tk))],
        out_specs=pl.BlockSpec((1, tq, 2 * HEAD_DIM), lambda bb, p, i: (bb, i, p)),
        out_shape=jax.ShapeDtypeStruct((b, s, D_ATT), BF16),
        scratch_shapes=[pltpu.VMEM((2 * tq, 2 * HEAD_DIM), F32),
                        pltpu.VMEM((2 * tq, LANES), F32),
                        pltpu.VMEM((4, 2, 2 * tq, tk), F32),
                        pltpu.VMEM((2, 2, 2 * tq, tk), F32),
                        pltpu.VMEM((2, 2, 2 * tq, LANES), F32)],
        compiler_params=pltpu.CompilerParams(
            dimension_semantics=("arbitrary", "arbitrary", "arbitrary"),
            vmem_limit_bytes=VMEM_LIMIT_BYTES),
        name="prompt_attn",
    )(bias2, q, k, v, _tri(tk))


def _sample_attn_kernel(pt_ref, qbd_ref, knt_ref, vnt_ref, tri_ref, bias_ref, *refs):
    del pt_ref
    npg = SAMPLE_PAGES_PER_STEP
    k_refs, v_refs = refs[:npg], refs[npg:2 * npg]
    o_ref, acc_ref, r_ref = refs[2 * npg:]
    n = pl.program_id(1)
    hq = qbd_ref.shape[1]
    nq = hq // N_HEADS
    qbd = qbd_ref[0]
    bias = bias_ref[...]
    tri = tri_ref[...]

    def step(pages, mask):
        logits_fns = [lambda kt=kt: _dot(qbd, kt()) + bias for kt, _ in pages]
        value_fns = [lambda w, vt=vt: _dot_nt(w, vt()) for _, vt in pages]
        _stick_breaking_step(logits_fns, value_fns, tri, [mask] * len(pages), acc_ref, r_ref)

    @pl.when(n == 0)
    def _():
        acc_ref[...] = jnp.zeros_like(acc_ref)
        r_ref[...] = jnp.zeros_like(r_ref)
        kpos = lax.broadcasted_iota(jnp.int32, (hq, PAGE_SIZE), 1)
        qpos = lax.broadcasted_iota(jnp.int32, (hq, PAGE_SIZE), 0) % nq
        step([(lambda: knt_ref[0], lambda: vnt_ref[0])], kpos < qpos)

    def page(ref):
        return lambda: ref[0, 0].reshape(N_HEADS * HEAD_DIM, PAGE_SIZE)

    step([(page(k), page(v)) for k, v in zip(k_refs, v_refs)], None)

    @pl.when(n == pl.num_programs(1) - 1)
    def _():
        acc = acc_ref[...]
        for h in range(N_HEADS):
            o_ref[0, h * nq:(h + 1) * nq, :] = acc[h * nq:(h + 1) * nq,
                                                   h * HEAD_DIM:(h + 1) * HEAD_DIM]


def _sample_attn(page_table, qbd, knt, vnt, cache_kt, cache_vt, bias_col):
    bd, n_pages = page_table.shape
    hq = qbd.shape[1]
    npg = SAMPLE_PAGES_PER_STEP
    assert n_pages % npg == 0

    def page_spec(p):
        return pl.BlockSpec(
            (1, 1, N_HEADS, HEAD_DIM, PAGE_SIZE),
            lambda b, n, pt: (0, pt[b, n_pages - 1 - (n * npg + p)], 0, 0, 0))

    per_req = lambda shape: pl.BlockSpec((1,) + shape, lambda b, n, pt: (b, 0, 0))
    const = lambda shape: pl.BlockSpec(shape, lambda b, n, pt: (0, 0))
    pages = [page_spec(p) for p in range(npg)]
    grid_spec = pltpu.PrefetchScalarGridSpec(
        num_scalar_prefetch=1,
        grid=(bd, n_pages // npg),
        in_specs=[per_req((hq, D_ATT)), per_req((D_ATT, PAGE_SIZE)), per_req((D_ATT, PAGE_SIZE)),
                  const((PAGE_SIZE, PAGE_SIZE)), const((hq, 1))] + pages + pages,
        out_specs=per_req((hq, HEAD_DIM)),
        scratch_shapes=[pltpu.VMEM((hq, D_ATT), F32), pltpu.VMEM((hq, LANES), F32)],
    )
    return pl.pallas_call(
        _sample_attn_kernel,
        grid_spec=grid_spec,
        out_shape=jax.ShapeDtypeStruct((bd, hq, HEAD_DIM), F32),
        compiler_params=pltpu.CompilerParams(
            dimension_semantics=("arbitrary", "arbitrary"), vmem_limit_bytes=VMEM_LIMIT_BYTES),
        name="sample_attn",
    )(page_table, qbd, knt, vnt, _tri(PAGE_SIZE), bias_col,
      *([cache_kt] * npg), *([cache_vt] * npg))


def _output_kernel(x_ref, o_ref, ma_ref, gb_ref, wpa_ref, wout_ref, n2_ref, wup_ref, wdown_ref,
                   nf_ref, y_ref):
    y_att = _dot(o_ref[...], wpa_ref[...])
    merged = ma_ref[...] + gb_ref[...] * y_att
    x1 = x_ref[...] + _dot(merged.astype(BF16), wout_ref[...])
    h2 = _rmsnorm(x1, n2_ref[...]).astype(BF16)
    x2 = x1
    ff_chunk = D_MODEL
    for c in range(0, D_FF, ff_chunk):
        f = jnp.maximum(_dot(h2, wup_ref[:, c:c + ff_chunk]), 0.0)
        x2 = x2 + _dot((f * f).astype(BF16), wdown_ref[c:c + ff_chunk, :])
    y_ref[...] = _rmsnorm(x2, nf_ref[...])


def _output(x, o, ma, gb, wpa, wout, n2, wup, wdown, nf, tm):
    n = x.shape[0]
    row = lambda d: pl.BlockSpec((tm, d), lambda i: (i, 0))
    return pl.pallas_call(
        _output_kernel,
        grid=(n // tm,),
        in_specs=[row(D_MODEL), row(D_ATT), row(D_MODEL), row(D_MODEL),
                  _const_spec((D_ATT, D_MODEL)), _const_spec((D_MODEL, D_MODEL)),
                  _const_spec((1, D_MODEL)), _const_spec((D_MODEL, D_FF)),
                  _const_spec((D_FF, D_MODEL)), _const_spec((1, D_MODEL))],
        out_specs=row(D_MODEL),
        out_shape=jax.ShapeDtypeStruct((n, D_MODEL), F32),
        compiler_params=pltpu.CompilerParams(
            dimension_semantics=("arbitrary",), vmem_limit_bytes=VMEM_LIMIT_BYTES),
        name="output_mlp",
    )(x, o, ma, gb, wpa, wout, n2, wup, wdown, nf)


def kernel(x_prompt, x_sample, cache_k, cache_v, state_conv, page_table, norm1_w, w_in, conv_w,
           w_proj_conv, w_proj_attn, att_bias, w_out, norm2_w, w_up, w_down, norm_f_w):
    depth = w_in.shape[0]
    assert depth == 1, "single-layer stack"
    bp, s, _ = x_prompt.shape
    bd, ls, _ = x_sample.shape

    n1 = norm1_w[0][None, :]
    n2 = norm2_w[0][None, :]
    nf = norm_f_w[None, :]
    win = w_in[0].astype(BF16)
    wpc = w_proj_conv[0].astype(BF16)
    wpa = w_proj_attn[0].astype(BF16)
    wout = w_out[0].astype(BF16)
    wup = w_up[0].astype(BF16)
    wdown = w_down[0].astype(BF16)
    convw = conv_w[0]
    bias2 = att_bias[0] * LOG2E

    xp = x_prompt.reshape(bp * s, D_MODEL)
    qp, kbp, vbp, kp, vp, cup, map_, gbp = _inproj_prompt(xp, n1, win, convw, wpc, s)
    op = _prompt_attn(qp.reshape(bp, s, D_ATT), kbp.reshape(bp, s, D_ATT),
                      vbp.reshape(bp, s, D_ATT), bias2)
    yp = _output(xp, op.reshape(bp * s, D_ATT), map_, gbp, wpa, wout, n2, wup, wdown, nf,
                 TOKEN_TILE)

    xs = x_sample.reshape(bd * ls, D_MODEL)
    st = state_conv[0]
    zeros = jnp.zeros((bd, ls, D_CONV), F32)
    p1 = zeros.at[:, 0].set(st[:, 1]).reshape(bd * ls, D_CONV)
    p2 = zeros.at[:, 0].set(st[:, 0]).at[:, 1].set(st[:, 1]).reshape(bd * ls, D_CONV)
    qs, _, _, ks, vs, cus, mas, gbs = _inproj_sample(xs, n1, win, convw, wpc, p1, p2, ls)
    hq = N_HEADS * ls
    q4 = qs.reshape(bd, ls, N_HEADS, HEAD_DIM).transpose(0, 2, 1, 3)
    eye = jnp.eye(N_HEADS, dtype=BF16)
    qbd = (q4[:, :, :, None, :] * eye[None, :, None, :, None]).reshape(bd, hq, D_ATT)
    bias_col = jnp.repeat(bias2, ls)[:, None]
    pad = ((0, 0), (0, 0), (0, PAGE_SIZE - ls))
    knt = jnp.pad(ks.reshape(bd, ls, D_ATT).transpose(0, 2, 1), pad)
    vnt = jnp.pad(vs.reshape(bd, ls, D_ATT).transpose(0, 2, 1), pad)
    cache_kt = cache_k.transpose(0, 1, 3, 4, 2)
    cache_vt = cache_v.transpose(0, 1, 3, 4, 2)
    os_ = _sample_attn(page_table, qbd, knt, vnt, cache_kt, cache_vt, bias_col)
    os_ = os_.reshape(bd, N_HEADS, ls, HEAD_DIM).transpose(0, 2, 1, 3).reshape(bd * ls, D_ATT)
    ys = _output(xs, os_.astype(BF16), mas, gbs, wpa, wout, n2, wup, wdown, nf, bd * ls)

    def paged(x):
        x = x.reshape(depth, bp, s // PAGE_SIZE, N_HEADS, HEAD_DIM, PAGE_SIZE)
        return x.transpose(0, 1, 2, 5, 3, 4)

    return (
        yp.reshape(bp, s, D_MODEL),
        ys.reshape(bd, ls, D_MODEL),
        paged(kp),
        paged(vp),
        cup.reshape(bp, s, D_CONV)[None, :, s - 2:, :],
        ks.reshape(depth, bd, ls, N_HEADS, HEAD_DIM),
        vs.reshape(depth, bd, ls, N_HEADS, HEAD_DIM),
        cus.reshape(bd, ls, D_CONV)[None, :, ls - 2:, :],
    )
```

```python
import functools
import math

import jax
import jax.numpy as jnp
from jax import lax
from jax.experimental import pallas as pl
from jax.experimental.pallas import tpu as pltpu

D_MODEL = 1024
D_CONV = 512
N_HEADS = 8
HEAD_DIM = 64
D_ATT = N_HEADS * HEAD_DIM
D_FF = 4 * D_MODEL
D_IN = 3 * D_CONV + 3 * D_ATT + 2 * D_MODEL
PAGE_SIZE = 128
LANES = 128
RMS_EPS = 1e-6
LOG2E = math.log2(math.e)

VMEM_LIMIT_BYTES = 56 * 1024 * 1024

TOKEN_TILE = 512
ATT_Q_TILE = 512
ATT_K_TILE = 256
SAMPLE_PAGES_PER_STEP = 32

F32 = jnp.float32
BF16 = jnp.bfloat16


def _dot(a, b):
    return lax.dot_general(a, b, (((1,), (0,)), ((), ())), preferred_element_type=F32)


def _dot_nt(a, b):
    return lax.dot_general(a, b, (((1,), (1,)), ((), ())), preferred_element_type=F32)


def _dot_halves(a, b):
    h = a.shape[0] // 2
    return jnp.concatenate([_dot(a[:h], b), _dot(a[h:], b)], axis=0)


def _rmsnorm(x, w):
    return x * lax.rsqrt(jnp.mean(x * x, axis=-1, keepdims=True) + RMS_EPS) * w


def _softplus2(z):
    neg_abs = lax.bitcast_convert_type(
        lax.bitcast_convert_type(z, jnp.uint32) | jnp.uint32(0x80000000), F32)
    return jnp.maximum(z, 0.0) + jnp.log(1.0 + jnp.exp2(neg_abs)) * LOG2E


def _softplus2_operand(z, mask):
    sp = _softplus2(z)
    if mask is not None:
        sp = jnp.where(mask, sp, 0.0)
    return sp


def _weights(z, cs, r, mask):
    r_full = jnp.concatenate([r] * (z.shape[1] // LANES), axis=1) if z.shape[1] > LANES else r
    w = jnp.exp2(z - cs - r_full)
    if mask is not None:
        w = jnp.where(mask, w, 0.0)
    return w


def _row_total(cs):
    return jnp.broadcast_to(cs[:, 0:1], (cs.shape[0], LANES))


def _tri(n):
    j = lax.broadcasted_iota(jnp.int32, (n, n), 0)
    s = lax.broadcasted_iota(jnp.int32, (n, n), 1)
    return (j >= s).astype(BF16)


def _const_spec(shape):
    return pl.BlockSpec(shape, lambda *_: (0,) * len(shape))


def _inproj_body(x_ref, n1_ref, win_ref, convw_ref, wpc_ref, prev_fn,
                 q_ref, kb_ref, vb_ref, k_ref, v_ref, cu_ref, ma_ref, gb_ref):
    x = x_ref[...]
    hb = _rmsnorm(x, n1_ref[...]).astype(BF16)

    def proj(lo, hi):
        return _dot_halves(hb, win_ref[:, lo:hi])

    o = 0
    b_g = proj(o, o + D_CONV); o += D_CONV
    c_g = proj(o, o + D_CONV); o += D_CONV
    u = proj(o, o + D_CONV); o += D_CONV
    cu = c_g * u
    cu_ref[...] = cu
    prev1, prev2 = prev_fn(cu)
    conv = convw_ref[0:1, :] * prev2 + convw_ref[1:2, :] * prev1 + convw_ref[2:3, :] * cu
    gated = (b_g * conv).astype(BF16)

    q = proj(o, o + D_ATT); o += D_ATT
    q_ref[...] = (q * (LOG2E * HEAD_DIM ** -0.5)).astype(BF16)
    k = proj(o, o + D_ATT); o += D_ATT
    _store_kv(k_ref, k)
    kb_ref[...] = k.astype(BF16)
    v = proj(o, o + D_ATT); o += D_ATT
    _store_kv(v_ref, v)
    vb_ref[...] = v.astype(BF16)
    y_conv = _dot_halves(gated, wpc_ref[...])
    gate_conv = proj(o, o + D_MODEL); o += D_MODEL
    ma_ref[...] = jax.nn.sigmoid(gate_conv) * y_conv
    gate_att = proj(o, o + D_MODEL); o += D_MODEL
    gb_ref[...] = jax.nn.sigmoid(gate_att)


def _inproj_prompt_kernel(tiles_per_seq, x_ref, n1_ref, win_ref, convw_ref, wpc_ref,
                          q_ref, kb_ref, vb_ref, k_ref, v_ref, cu_ref, ma_ref, gb_ref, carry_ref):
    tm = x_ref.shape[0]

    @pl.when(pl.program_id(0) % tiles_per_seq == 0)
    def _():
        carry_ref[...] = jnp.zeros_like(carry_ref)

    def prev_fn(cu):
        row = lax.broadcasted_iota(jnp.int32, cu.shape, 0)
        c1 = carry_ref[7:8, :]
        c2 = carry_ref[6:7, :]
        prev1 = jnp.where(row == 0, c1, pltpu.roll(cu, 1, 0))
        prev2 = jnp.where(row == 0, c2, jnp.where(row == 1, c1, pltpu.roll(cu, 2, 0)))
        carry_ref[...] = cu[tm - 8:, :]
        return prev1, prev2

    _inproj_body(x_ref, n1_ref, win_ref, convw_ref, wpc_ref, prev_fn,
                 q_ref, kb_ref, vb_ref, k_ref, v_ref, cu_ref, ma_ref, gb_ref)


def _inproj_sample_kernel(seq, x_ref, n1_ref, win_ref, convw_ref, wpc_ref, p1_ref, p2_ref,
                          q_ref, kb_ref, vb_ref, k_ref, v_ref, cu_ref, ma_ref, gb_ref):
    def prev_fn(cu):
        pos = lax.broadcasted_iota(jnp.int32, cu.shape, 0) % seq
        prev1 = jnp.where(pos == 0, p1_ref[...], pltpu.roll(cu, 1, 0))
        prev2 = jnp.where(pos < 2, p2_ref[...], pltpu.roll(cu, 2, 0))
        return prev1, prev2

    _inproj_body(x_ref, n1_ref, win_ref, convw_ref, wpc_ref, prev_fn,
                 q_ref, kb_ref, vb_ref, k_ref, v_ref, cu_ref, ma_ref, gb_ref)


def _store_kv(ref, x):
    if len(ref.shape) == 2:
        ref[...] = x
    else:
        for p in range(ref.shape[0]):
            ref[p] = x[p * PAGE_SIZE:(p + 1) * PAGE_SIZE, :].T


def _inproj_out(n, tm, paged_kv):
    row = lambda d: pl.BlockSpec((tm, d), lambda i: (i, 0))
    if paged_kv:
        kv_shape = jax.ShapeDtypeStruct((n // PAGE_SIZE, D_ATT, PAGE_SIZE), F32)
        kv_spec = pl.BlockSpec((tm // PAGE_SIZE, D_ATT, PAGE_SIZE), lambda i: (i, 0, 0))
    else:
        kv_shape = jax.ShapeDtypeStruct((n, D_ATT), F32)
        kv_spec = row(D_ATT)
    shapes = [
        jax.ShapeDtypeStruct((n, D_ATT), BF16),
        jax.ShapeDtypeStruct((n, D_ATT), BF16),
        jax.ShapeDtypeStruct((n, D_ATT), BF16),
        kv_shape,
        kv_shape,
        jax.ShapeDtypeStruct((n, D_CONV), F32),
        jax.ShapeDtypeStruct((n, D_MODEL), F32),
        jax.ShapeDtypeStruct((n, D_MODEL), F32),
    ]
    specs = [row(D_ATT)] * 3 + [kv_spec] * 2 + [row(D_CONV), row(D_MODEL), row(D_MODEL)]
    return shapes, specs


def _inproj_weight_specs():
    return [_const_spec((1, D_MODEL)), _const_spec((D_MODEL, D_IN)),
            _const_spec((3, D_CONV)), _const_spec((D_CONV, D_MODEL))]


def _inproj_prompt(x, n1, win, convw, wpc, seq_len):
    n = x.shape[0]
    tm = TOKEN_TILE
    shapes, specs = _inproj_out(n, tm, paged_kv=True)
    return pl.pallas_call(
        functools.partial(_inproj_prompt_kernel, seq_len // tm),
        grid=(n // tm,),
        in_specs=[pl.BlockSpec((tm, D_MODEL), lambda i: (i, 0))] + _inproj_weight_specs(),
        out_specs=specs,
        out_shape=shapes,
        scratch_shapes=[pltpu.VMEM((8, D_CONV), F32)],
        compiler_params=pltpu.CompilerParams(
            dimension_semantics=("arbitrary",), vmem_limit_bytes=VMEM_LIMIT_BYTES),
        name="inproj_prompt",
    )(x, n1, win, convw, wpc)


def _inproj_sample(x, n1, win, convw, wpc, p1, p2, seq):
    n = x.shape[0]
    shapes, specs = _inproj_out(n, n, paged_kv=False)
    full = lambda d: pl.BlockSpec((n, d), lambda i: (0, 0))
    return pl.pallas_call(
        functools.partial(_inproj_sample_kernel, seq),
        grid=(1,),
        in_specs=[full(D_MODEL)] + _inproj_weight_specs() + [full(D_CONV), full(D_CONV)],
        out_specs=specs,
        out_shape=shapes,
        compiler_params=pltpu.CompilerParams(
            dimension_semantics=("arbitrary",), vmem_limit_bytes=VMEM_LIMIT_BYTES),
        name="inproj_sample",
    )(x, n1, win, convw, wpc, p1, p2)


def _stick_breaking_step(logits_fns, value_fns, tri, masks, acc_ref, r_ref):
    zs = [f() for f in logits_fns]
    sums = [_dot(_softplus2_operand(z, m), tri) for z, m in zip(zs, masks)]
    r = r_ref[...]
    acc = acc_ref[...]
    for z, cs, m, value_fn in zip(zs, sums, masks, value_fns):
        acc = acc + value_fn(_weights(z, cs, r, m))
        r = r + _row_total(cs)
    acc_ref[...] = acc
    r_ref[...] = r


def _prompt_attn_kernel(bias_ref, q_ref, k_ref, v_ref, tri_ref, o_ref,
                        acc_ref, r_ref, z_buf, c_buf, t_buf):
    tq, tk = ATT_Q_TILE, ATT_K_TILE
    pair = pl.program_id(1)
    i = pl.program_id(2)

    assert tq == 2 * tk
    qq = q_ref[0]
    lane = lax.broadcasted_iota(jnp.int32, qq.shape, 1)
    zero = jnp.zeros_like(qq)
    qa = jnp.where(lane < HEAD_DIM, qq, zero)
    qb = jnp.where(lane >= HEAD_DIM, qq, zero)
    qs = jnp.concatenate([qa[:tk], qb[:tk], qa[tk:], qb[tk:]], axis=0)
    row = lax.broadcasted_iota(jnp.int32, (2 * tq, 1), 0)
    bias = jnp.where((row // tk) % 2 == 0, bias_ref[2 * pair], bias_ref[2 * pair + 1])
    tri = tri_ref[...]

    def keys(j):
        return k_ref[0, pl.ds(pl.multiple_of(j * tk, tk), tk), :]

    def values(j):
        return v_ref[0, pl.ds(pl.multiple_of(j * tk, tk), tk), :]

    def logits(j):
        return _dot_nt(qs, keys(j)) + bias

    first_diag = 2 * i
    n_pairs = i
    top = first_diag - 1

    def pair(m):
        return [top - 2 * m - sub for sub in range(2)]

    def stage_logits(m, zslot):
        for sub, j in enumerate(pair(m)):
            z_buf[zslot, sub] = logits(jnp.maximum(j, 0))

    def stage_sums(zslot, cslot):
        for sub in range(2):
            cs = _dot(_softplus2_operand(z_buf[zslot, sub], None), tri)
            c_buf[cslot, sub] = cs
            t_buf[cslot, sub] = _row_total(cs)

    def stage_values(m, zslot, cslot):
        r = r_ref[...]
        acc = acc_ref[...]
        for sub, j in enumerate(pair(m)):
            w = _weights(z_buf[zslot, sub], c_buf[cslot, sub], r, None)
            acc = acc + _dot(w, values(j))
            r = r + t_buf[cslot, sub]
        acc_ref[...] = acc
        r_ref[...] = r

    def step(t, phase, more_pairs=True):
        za, zb, zc = (phase + 1) % 4, phase, (phase - 1) % 4
        cb, cc = phase % 2, (phase - 1) % 2
        r = r_ref[...]
        acc = acc_ref[...]
        for sub in range(2):
            w = _weights(z_buf[zc, sub], c_buf[cc, sub], r, None)
            acc = acc + _dot(w, values(pair(t - 1)[sub]))
            r = r + t_buf[cc, sub]
            cs = _dot(_softplus2_operand(z_buf[zb, sub], None), tri)
            c_buf[cb, sub] = cs
            t_buf[cb, sub] = _row_total(cs)
            if more_pairs:
                z_buf[za, sub] = logits(jnp.maximum(pair(t + 1)[sub], 0))
        acc_ref[...] = acc
        r_ref[...] = r

    triangle = (lax.broadcasted_iota(jnp.int32, (tq, tk), 1)
                < lax.broadcasted_iota(jnp.int32, (tq, tk), 0) % tk)
    z_hi = _dot_nt(qs[tq:], keys(first_diag + 1)) + bias[tq:]
    z_lo = logits(first_diag)
    stage_logits(0, 0)
    cs_hi = _dot(_softplus2_operand(z_hi, triangle), tri)
    sp_lo = jnp.concatenate([_softplus2_operand(z_lo[:tq], triangle),
                             _softplus2_operand(z_lo[tq:], None)], axis=0)
    cs_lo = _dot(sp_lo, tri)
    stage_logits(1, 1)
    no_total = jnp.zeros((tq, LANES), F32)
    pv_hi = _dot(_weights(z_hi, cs_hi, no_total, triangle), values(first_diag + 1))
    r_hi = jnp.concatenate([no_total, _row_total(cs_hi)], axis=0)
    w_lo = jnp.concatenate([_weights(z_lo[:tq], cs_lo[:tq], r_hi[:tq], triangle),
                            _weights(z_lo[tq:], cs_lo[tq:], r_hi[tq:], None)], axis=0)
    acc_ref[...] = (_dot(w_lo, values(first_diag))
                    + jnp.concatenate([jnp.zeros_like(pv_hi), pv_hi], axis=0))
    r_ref[...] = r_hi + _row_total(cs_lo)
    stage_sums(0, 0)

    @pl.when(n_pairs >= 1)
    def _():
        last = n_pairs - 1
        n_quads = lax.shift_right_logical(last, 2)

        def steps(t, count, ends_at_last=False):
            for d in range(count):
                step(t + d, (1 + d) % 4, more_pairs=not (ends_at_last and d == count - 1))

        def body(u, carry):
            steps(1 + 4 * u, 4)
            return carry

        lax.fori_loop(0, n_quads, body, 0)
        for rem in range(4):
            @pl.when((last & 3) == rem)
            def _(rem=rem):
                steps(1 + 4 * n_quads, rem, ends_at_last=True)
                stage_values(last, rem, rem % 2)

    acc = acc_ref[...]
    first_head = lax.broadcasted_iota(jnp.int32, (tk, 2 * HEAD_DIM), 1) < HEAD_DIM
    for half in range(2):
        rows = acc[2 * half * tk:2 * (half + 1) * tk]
        o_ref[0, half * tk:(half + 1) * tk, :] = jnp.where(
            first_head, rows[:tk], rows[tk:]).astype(o_ref.dtype)


def _prompt_attn(q, k, v, bias2):
    b, s, _ = q.shape
    tq, tk = ATT_Q_TILE, ATT_K_TILE
    assert tq % tk == 0 and s % tq == 0
    kv_spec = pl.BlockSpec((1, s, 2 * HEAD_DIM), lambda bb, p, i: (bb, 0, p))
    return pl.pallas_call(
        _prompt_attn_kernel,
        grid=(b, N_HEADS // 2, s // tq),
        in_specs=[pl.BlockSpec(memory_space=pltpu.SMEM),
                  pl.BlockSpec((1, tq, 2 * HEAD_DIM), lambda bb, p, i: (bb, i, p)),
                  kv_spec, kv_spec,
                  _const_spec((tk, tk))],
        out_specs=pl.BlockSpec((1, tq, 2 * HEAD_DIM), lambda bb, p, i: (bb, i, p)),
        out_shape=jax.ShapeDtypeStruct((b, s, D_ATT), BF16),
        scratch_shapes=[pltpu.VMEM((2 * tq, 2 * HEAD_DIM), F32),
                        pltpu.VMEM((2 * tq, LANES), F32),
                        pltpu.VMEM((4, 2, 2 * tq, tk), F32),
                        pltpu.VMEM((2, 2, 2 * tq, tk), F32),
                        pltpu.VMEM((2, 2, 2 * tq, LANES), F32)],
        compiler_params=pltpu.CompilerParams(
            dimension_semantics=("arbitrary", "arbitrary", "arbitrary"),
            vmem_limit_bytes=VMEM_LIMIT_BYTES),
        name="prompt_attn",
    )(bias2, q, k, v, _tri(tk))


def _sample_attn_kernel(pt_ref, qbd_ref, knt_ref, vnt_ref, tri_ref, bias_ref, *refs):
    del pt_ref
    npg = SAMPLE_PAGES_PER_STEP
    k_refs, v_refs = refs[:npg], refs[npg:2 * npg]
    o_ref, acc_ref, r_ref = refs[2 * npg:]
    n = pl.program_id(1)
    hq = qbd_ref.shape[1]
    nq = hq // N_HEADS
    qbd = qbd_ref[0]
    bias = bias_ref[...]
    tri = tri_ref[...]

    def step(pages, mask):
        logits_fns = [lambda kt=kt: _dot(qbd, kt()) + bias for kt, _ in pages]
        value_fns = [lambda w, vt=vt: _dot_nt(w, vt()) for _, vt in pages]
        _stick_breaking_step(logits_fns, value_fns, tri, [mask] * len(pages), acc_ref, r_ref)

    @pl.when(n == 0)
    def _():
        acc_ref[...] = jnp.zeros_like(acc_ref)
        r_ref[...] = jnp.zeros_like(r_ref)
        kpos = lax.broadcasted_iota(jnp.int32, (hq, PAGE_SIZE), 1)
        qpos = lax.broadcasted_iota(jnp.int32, (hq, PAGE_SIZE), 0) % nq
        step([(lambda: knt_ref[0], lambda: vnt_ref[0])], kpos < qpos)

    def page(ref):
        return lambda: ref[0, 0].reshape(N_HEADS * HEAD_DIM, PAGE_SIZE)

    step([(page(k), page(v)) for k, v in zip(k_refs, v_refs)], None)

    @pl.when(n == pl.num_programs(1) - 1)
    def _():
        acc = acc_ref[...]
        for h in range(N_HEADS):
            o_ref[0, h * nq:(h + 1) * nq, :] = acc[h * nq:(h + 1) * nq,
                                                   h * HEAD_DIM:(h + 1) * HEAD_DIM]


def _sample_attn(page_table, qbd, knt, vnt, cache_kt, cache_vt, bias_col):
    bd, n_pages = page_table.shape
    hq = qbd.shape[1]
    npg = SAMPLE_PAGES_PER_STEP
    assert n_pages % npg == 0

    def page_spec(p):
        return pl.BlockSpec(
            (1, 1, N_HEADS, HEAD_DIM, PAGE_SIZE),
            lambda b, n, pt: (0, pt[b, n_pages - 1 - (n * npg + p)], 0, 0, 0))

    per_req = lambda shape: pl.BlockSpec((1,) + shape, lambda b, n, pt: (b, 0, 0))
    const = lambda shape: pl.BlockSpec(shape, lambda b, n, pt: (0, 0))
    pages = [page_spec(p) for p in range(npg)]
    grid_spec = pltpu.PrefetchScalarGridSpec(
        num_scalar_prefetch=1,
        grid=(bd, n_pages // npg),
        in_specs=[per_req((hq, D_ATT)), per_req((D_ATT, PAGE_SIZE)), per_req((D_ATT, PAGE_SIZE)),
                  const((PAGE_SIZE, PAGE_SIZE)), const((hq, 1))] + pages + pages,
        out_specs=per_req((hq, HEAD_DIM)),
        scratch_shapes=[pltpu.VMEM((hq, D_ATT), F32), pltpu.VMEM((hq, LANES), F32)],
    )
    return pl.pallas_call(
        _sample_attn_kernel,
        grid_spec=grid_spec,
        out_shape=jax.ShapeDtypeStruct((bd, hq, HEAD_DIM), F32),
        compiler_params=pltpu.CompilerParams(
            dimension_semantics=("arbitrary", "arbitrary"), vmem_limit_bytes=VMEM_LIMIT_BYTES),
        name="sample_attn",
    )(page_table, qbd, knt, vnt, _tri(PAGE_SIZE), bias_col,
      *([cache_kt] * npg), *([cache_vt] * npg))


def _output_kernel(x_ref, o_ref, ma_ref, gb_ref, wpa_ref, wout_ref, n2_ref, wup_ref, wdown_ref,
                   nf_ref, y_ref):
    y_att = _dot_halves(o_ref[...], wpa_ref[...])
    merged = ma_ref[...] + gb_ref[...] * y_att
    x1 = x_ref[...] + _dot_halves(merged.astype(BF16), wout_ref[...])
    h2 = _rmsnorm(x1, n2_ref[...]).astype(BF16)
    x2 = x1
    ff_chunk = D_MODEL
    for c in range(0, D_FF, ff_chunk):
        f = jnp.maximum(_dot_halves(h2, wup_ref[:, c:c + ff_chunk]), 0.0)
        x2 = x2 + _dot_halves((f * f).astype(BF16), wdown_ref[c:c + ff_chunk, :])
    y_ref[...] = _rmsnorm(x2, nf_ref[...])


def _output(x, o, ma, gb, wpa, wout, n2, wup, wdown, nf, tm):
    n = x.shape[0]
    row = lambda d: pl.BlockSpec((tm, d), lambda i: (i, 0))
    return pl.pallas_call(
        _output_kernel,
        grid=(n // tm,),
        in_specs=[row(D_MODEL), row(D_ATT), row(D_MODEL), row(D_MODEL),
                  _const_spec((D_ATT, D_MODEL)), _const_spec((D_MODEL, D_MODEL)),
                  _const_spec((1, D_MODEL)), _const_spec((D_MODEL, D_FF)),
                  _const_spec((D_FF, D_MODEL)), _const_spec((1, D_MODEL))],
        out_specs=row(D_MODEL),
        out_shape=jax.ShapeDtypeStruct((n, D_MODEL), F32),
        compiler_params=pltpu.CompilerParams(
            dimension_semantics=("arbitrary",), vmem_limit_bytes=VMEM_LIMIT_BYTES),
        name="output_mlp",
    )(x, o, ma, gb, wpa, wout, n2, wup, wdown, nf)


def kernel(x_prompt, x_sample, cache_k, cache_v, state_conv, page_table, norm1_w, w_in, conv_w,
           w_proj_conv, w_proj_attn, att_bias, w_out, norm2_w, w_up, w_down, norm_f_w):
    depth = w_in.shape[0]
    assert depth == 1, "single-layer stack"
    bp, s, _ = x_prompt.shape
    bd, ls, _ = x_sample.shape

    n1 = norm1_w[0][None, :]
    n2 = norm2_w[0][None, :]
    nf = norm_f_w[None, :]
    win = w_in[0].astype(BF16)
    wpc = w_proj_conv[0].astype(BF16)
    wpa = w_proj_attn[0].astype(BF16)
    wout = w_out[0].astype(BF16)
    wup = w_up[0].astype(BF16)
    wdown = w_down[0].astype(BF16)
    convw = conv_w[0]
    bias2 = att_bias[0] * LOG2E

    xp = x_prompt.reshape(bp * s, D_MODEL)
    qp, kbp, vbp, kp, vp, cup, map_, gbp = _inproj_prompt(xp, n1, win, convw, wpc, s)
    op = _prompt_attn(qp.reshape(bp, s, D_ATT), kbp.reshape(bp, s, D_ATT),
                      vbp.reshape(bp, s, D_ATT), bias2)
    yp = _output(xp, op.reshape(bp * s, D_ATT), map_, gbp, wpa, wout, n2, wup, wdown, nf,
                 TOKEN_TILE)

    xs = x_sample.reshape(bd * ls, D_MODEL)
    st = state_conv[0]
    zeros = jnp.zeros((bd, ls, D_CONV), F32)
    p1 = zeros.at[:, 0].set(st[:, 1]).reshape(bd * ls, D_CONV)
    p2 = zeros.at[:, 0].set(st[:, 0]).at[:, 1].set(st[:, 1]).reshape(bd * ls, D_CONV)
    qs, _, _, ks, vs, cus, mas, gbs = _inproj_sample(xs, n1, win, convw, wpc, p1, p2, ls)
    hq = N_HEADS * ls
    q4 = qs.reshape(bd, ls, N_HEADS, HEAD_DIM).transpose(0, 2, 1, 3)
    eye = jnp.eye(N_HEADS, dtype=BF16)
    qbd = (q4[:, :, :, None, :] * eye[None, :, None, :, None]).reshape(bd, hq, D_ATT)
    bias_col = jnp.repeat(bias2, ls)[:, None]
    pad = ((0, 0), (0, 0), (0, PAGE_SIZE - ls))
    knt = jnp.pad(ks.reshape(bd, ls, D_ATT).transpose(0, 2, 1), pad)
    vnt = jnp.pad(vs.reshape(bd, ls, D_ATT).transpose(0, 2, 1), pad)
    cache_kt = cache_k.transpose(0, 1, 3, 4, 2)
    cache_vt = cache_v.transpose(0, 1, 3, 4, 2)
    os_ = _sample_attn(page_table, qbd, knt, vnt, cache_kt, cache_vt, bias_col)
    os_ = os_.reshape(bd, N_HEADS, ls, HEAD_DIM).transpose(0, 2, 1, 3).reshape(bd * ls, D_ATT)
    ys = _output(xs, os_.astype(BF16), mas, gbs, wpa, wout, n2, wup, wdown, nf, bd * ls)

    def paged(x):
        x = x.reshape(depth, bp, s // PAGE_SIZE, N_HEADS, HEAD_DIM, PAGE_SIZE)
        return x.transpose(0, 1, 2, 5, 3, 4)

    return (
        yp.reshape(bp, s, D_MODEL),
        ys.reshape(bd, ls, D_MODEL),
        paged(kp),
        paged(vp),
        cup.reshape(bp, s, D_CONV)[None, :, s - 2:, :],
        ks.reshape(depth, bd, ls, N_HEADS, HEAD_DIM),
        vs.reshape(depth, bd, ls, N_HEADS, HEAD_DIM),
        cus.reshape(bd, ls, D_CONV)[None, :, ls - 2:, :],
    )
```

```python
import functools
import math

import jax
import jax.numpy as jnp
from jax import lax
from jax.experimental import pallas as pl
from jax.experimental.pallas import tpu as pltpu

D_MODEL = 1024
D_CONV = 512
N_HEADS = 8
HEAD_DIM = 64
D_ATT = N_HEADS * HEAD_DIM
D_FF = 4 * D_MODEL
D_IN = 3 * D_CONV + 3 * D_ATT + 2 * D_MODEL
PAGE_SIZE = 128
LANES = 128
RMS_EPS = 1e-6
LOG2E = math.log2(math.e)

VMEM_LIMIT_BYTES = 56 * 1024 * 1024

TOKEN_TILE = 512
ATT_Q_TILE = 512
ATT_K_TILE = 256
SAMPLE_PAGES_PER_STEP = 32

F32 = jnp.float32
BF16 = jnp.bfloat16


def _dot(a, b):
    return lax.dot_general(a, b, (((1,), (0,)), ((), ())), preferred_element_type=F32)


def _dot_nt(a, b):
    return lax.dot_general(a, b, (((1,), (1,)), ((), ())), preferred_element_type=F32)


def _dot_halves(a, b):
    h = a.shape[0] // 2
    return jnp.concatenate([_dot(a[:h], b), _dot(a[h:], b)], axis=0)


def _rmsnorm(x, w):
    return x * lax.rsqrt(jnp.mean(x * x, axis=-1, keepdims=True) + RMS_EPS) * w


def _softplus2(z):
    neg_abs = lax.bitcast_convert_type(
        lax.bitcast_convert_type(z, jnp.uint32) | jnp.uint32(0x80000000), F32)
    return jnp.maximum(z, 0.0) + jnp.log(1.0 + jnp.exp2(neg_abs)) * LOG2E


def _softplus2_operand(z, mask):
    sp = _softplus2(z)
    if mask is not None:
        sp = jnp.where(mask, sp, 0.0)
    return sp


def _suffix_sums(sp, tri):
    return _dot(sp, tri) + sp


def _weights(z, cs, r, mask):
    r_full = jnp.concatenate([r] * (z.shape[1] // LANES), axis=1) if z.shape[1] > LANES else r
    w = jnp.exp2(z - cs - r_full)
    if mask is not None:
        w = jnp.where(mask, w, 0.0)
    return w


def _row_total(cs):
    return jnp.broadcast_to(cs[:, 0:1], (cs.shape[0], LANES))


def _tri(n):
    j = lax.broadcasted_iota(jnp.int32, (n, n), 0)
    s = lax.broadcasted_iota(jnp.int32, (n, n), 1)
    return (j > s).astype(BF16)


def _const_spec(shape):
    return pl.BlockSpec(shape, lambda *_: (0,) * len(shape))


def _inproj_body(x_ref, n1_ref, win_ref, convw_ref, wpc_ref, prev_fn,
                 q_ref, kb_ref, vb_ref, k_ref, v_ref, cu_ref, ma_ref, gb_ref):
    x = x_ref[...]
    hb = _rmsnorm(x, n1_ref[...]).astype(BF16)

    def proj(lo, hi):
        return _dot_halves(hb, win_ref[:, lo:hi])

    o = 0
    b_g = proj(o, o + D_CONV); o += D_CONV
    c_g = proj(o, o + D_CONV); o += D_CONV
    u = proj(o, o + D_CONV); o += D_CONV
    cu = c_g * u
    cu_ref[...] = cu
    prev1, prev2 = prev_fn(cu)
    conv = convw_ref[0:1, :] * prev2 + convw_ref[1:2, :] * prev1 + convw_ref[2:3, :] * cu
    gated = (b_g * conv).astype(BF16)

    q = proj(o, o + D_ATT); o += D_ATT
    q_ref[...] = (q * (LOG2E * HEAD_DIM ** -0.5)).astype(BF16)
    k = proj(o, o + D_ATT); o += D_ATT
    _store_kv(k_ref, k)
    kb_ref[...] = k.astype(BF16)
    v = proj(o, o + D_ATT); o += D_ATT
    _store_kv(v_ref, v)
    vb_ref[...] = v.astype(BF16)
    y_conv = _dot_halves(gated, wpc_ref[...])
    gate_conv = proj(o, o + D_MODEL); o += D_MODEL
    ma_ref[...] = jax.nn.sigmoid(gate_conv) * y_conv
    gate_att = proj(o, o + D_MODEL); o += D_MODEL
    gb_ref[...] = jax.nn.sigmoid(gate_att)


def _inproj_prompt_kernel(tiles_per_seq, x_ref, n1_ref, win_ref, convw_ref, wpc_ref,
                          q_ref, kb_ref, vb_ref, k_ref, v_ref, cu_ref, ma_ref, gb_ref, carry_ref):
    tm = x_ref.shape[0]

    @pl.when(pl.program_id(0) % tiles_per_seq == 0)
    def _():
        carry_ref[...] = jnp.zeros_like(carry_ref)

    def prev_fn(cu):
        row = lax.broadcasted_iota(jnp.int32, cu.shape, 0)
        c1 = carry_ref[7:8, :]
        c2 = carry_ref[6:7, :]
        prev1 = jnp.where(row == 0, c1, pltpu.roll(cu, 1, 0))
        prev2 = jnp.where(row == 0, c2, jnp.where(row == 1, c1, pltpu.roll(cu, 2, 0)))
        carry_ref[...] = cu[tm - 8:, :]
        return prev1, prev2

    _inproj_body(x_ref, n1_ref, win_ref, convw_ref, wpc_ref, prev_fn,
                 q_ref, kb_ref, vb_ref, k_ref, v_ref, cu_ref, ma_ref, gb_ref)


def _inproj_sample_kernel(seq, x_ref, n1_ref, win_ref, convw_ref, wpc_ref, p1_ref, p2_ref,
                          q_ref, kb_ref, vb_ref, k_ref, v_ref, cu_ref, ma_ref, gb_ref):
    def prev_fn(cu):
        pos = lax.broadcasted_iota(jnp.int32, cu.shape, 0) % seq
        prev1 = jnp.where(pos == 0, p1_ref[...], pltpu.roll(cu, 1, 0))
        prev2 = jnp.where(pos < 2, p2_ref[...], pltpu.roll(cu, 2, 0))
        return prev1, prev2

    _inproj_body(x_ref, n1_ref, win_ref, convw_ref, wpc_ref, prev_fn,
                 q_ref, kb_ref, vb_ref, k_ref, v_ref, cu_ref, ma_ref, gb_ref)


def _store_kv(ref, x):
    if len(ref.shape) == 2:
        ref[...] = x
    else:
        for p in range(ref.shape[0]):
            ref[p] = x[p * PAGE_SIZE:(p + 1) * PAGE_SIZE, :].T


def _inproj_out(n, tm, paged_kv):
    row = lambda d: pl.BlockSpec((tm, d), lambda i: (i, 0))
    if paged_kv:
        kv_shape = jax.ShapeDtypeStruct((n // PAGE_SIZE, D_ATT, PAGE_SIZE), F32)
        kv_spec = pl.BlockSpec((tm // PAGE_SIZE, D_ATT, PAGE_SIZE), lambda i: (i, 0, 0))
    else:
        kv_shape = jax.ShapeDtypeStruct((n, D_ATT), F32)
        kv_spec = row(D_ATT)
    shapes = [
        jax.ShapeDtypeStruct((n, D_ATT), BF16),
        jax.ShapeDtypeStruct((n, D_ATT), BF16),
        jax.ShapeDtypeStruct((n, D_ATT), BF16),
        kv_shape,
        kv_shape,
        jax.ShapeDtypeStruct((n, D_CONV), F32),
        jax.ShapeDtypeStruct((n, D_MODEL), F32),
        jax.ShapeDtypeStruct((n, D_MODEL), F32),
    ]
    specs = [row(D_ATT)] * 3 + [kv_spec] * 2 + [row(D_CONV), row(D_MODEL), row(D_MODEL)]
    return shapes, specs


def _inproj_weight_specs():
    return [_const_spec((1, D_MODEL)), _const_spec((D_MODEL, D_IN)),
            _const_spec((3, D_CONV)), _const_spec((D_CONV, D_MODEL))]


def _inproj_prompt(x, n1, win, convw, wpc, seq_len):
    n = x.shape[0]
    tm = TOKEN_TILE
    shapes, specs = _inproj_out(n, tm, paged_kv=True)
    return pl.pallas_call(
        functools.partial(_inproj_prompt_kernel, seq_len // tm),
        grid=(n // tm,),
        in_specs=[pl.BlockSpec((tm, D_MODEL), lambda i: (i, 0))] + _inproj_weight_specs(),
        out_specs=specs,
        out_shape=shapes,
        scratch_shapes=[pltpu.VMEM((8, D_CONV), F32)],
        compiler_params=pltpu.CompilerParams(
            dimension_semantics=("arbitrary",), vmem_limit_bytes=VMEM_LIMIT_BYTES),
        name="inproj_prompt",
    )(x, n1, win, convw, wpc)


def _inproj_sample(x, n1, win, convw, wpc, p1, p2, seq):
    n = x.shape[0]
    shapes, specs = _inproj_out(n, n, paged_kv=False)
    full = lambda d: pl.BlockSpec((n, d), lambda i: (0, 0))
    return pl.pallas_call(
        functools.partial(_inproj_sample_kernel, seq),
        grid=(1,),
        in_specs=[full(D_MODEL)] + _inproj_weight_specs() + [full(D_CONV), full(D_CONV)],
        out_specs=specs,
        out_shape=shapes,
        compiler_params=pltpu.CompilerParams(
            dimension_semantics=("arbitrary",), vmem_limit_bytes=VMEM_LIMIT_BYTES),
        name="inproj_sample",
    )(x, n1, win, convw, wpc, p1, p2)


def _stick_breaking_step(logits_fns, value_fns, tri, masks, acc_ref, r_ref):
    zs = [f() for f in logits_fns]
    sums = [_suffix_sums(_softplus2_operand(z, m), tri) for z, m in zip(zs, masks)]
    r = r_ref[...]
    acc = acc_ref[...]
    for z, cs, m, value_fn in zip(zs, sums, masks, value_fns):
        acc = acc + value_fn(_weights(z, cs, r, m))
        r = r + _row_total(cs)
    acc_ref[...] = acc
    r_ref[...] = r


def _prompt_attn_kernel(bias_ref, q_ref, k_ref, v_ref, tri_ref, o_ref,
                        acc_ref, r_ref, z_buf, c_buf, t_buf):
    tq, tk = ATT_Q_TILE, ATT_K_TILE
    pair = pl.program_id(1)
    i = pl.program_id(2)

    assert tq == 2 * tk
    qq = q_ref[0]
    lane = lax.broadcasted_iota(jnp.int32, qq.shape, 1)
    zero = jnp.zeros_like(qq)
    qa = jnp.where(lane < HEAD_DIM, qq, zero)
    qb = jnp.where(lane >= HEAD_DIM, qq, zero)
    qs = jnp.concatenate([qa[:tk], qb[:tk], qa[tk:], qb[tk:]], axis=0)
    row = lax.broadcasted_iota(jnp.int32, (2 * tq, 1), 0)
    bias = jnp.where((row // tk) % 2 == 0, bias_ref[2 * pair], bias_ref[2 * pair + 1])
    tri = tri_ref[...]

    def keys(j):
        return k_ref[0, pl.ds(pl.multiple_of(j * tk, tk), tk), :]

    def values(j):
        return v_ref[0, pl.ds(pl.multiple_of(j * tk, tk), tk), :]

    def logits(j):
        return _dot_nt(qs, keys(j)) + bias

    first_diag = 2 * i
    n_pairs = i
    top = first_diag - 1

    def pair(m):
        return [top - 2 * m - sub for sub in range(2)]

    def stage_logits(m, zslot):
        for sub, j in enumerate(pair(m)):
            z_buf[zslot, sub] = logits(jnp.maximum(j, 0))

    def stage_sums(zslot, cslot):
        for sub in range(2):
            cs = _suffix_sums(_softplus2_operand(z_buf[zslot, sub], None), tri)
            c_buf[cslot, sub] = cs
            t_buf[cslot, sub] = _row_total(cs)

    def stage_values(m, zslot, cslot):
        r = r_ref[...]
        acc = acc_ref[...]
        for sub, j in enumerate(pair(m)):
            w = _weights(z_buf[zslot, sub], c_buf[cslot, sub], r, None)
            acc = acc + _dot(w, values(j))
            r = r + t_buf[cslot, sub]
        acc_ref[...] = acc
        r_ref[...] = r

    def step(t, phase, more_pairs=True):
        za, zb, zc = (phase + 1) % 4, phase, (phase - 1) % 4
        cb, cc = phase % 2, (phase - 1) % 2
        r = r_ref[...]
        acc = acc_ref[...]
        for sub in range(2):
            w = _weights(z_buf[zc, sub], c_buf[cc, sub], r, None)
            acc = acc + _dot(w, values(pair(t - 1)[sub]))
            r = r + t_buf[cc, sub]
            cs = _suffix_sums(_softplus2_operand(z_buf[zb, sub], None), tri)
            c_buf[cb, sub] = cs
            t_buf[cb, sub] = _row_total(cs)
            if more_pairs:
                z_buf[za, sub] = logits(jnp.maximum(pair(t + 1)[sub], 0))
        acc_ref[...] = acc
        r_ref[...] = r

    triangle = (lax.broadcasted_iota(jnp.int32, (tq, tk), 1)
                < lax.broadcasted_iota(jnp.int32, (tq, tk), 0) % tk)
    z_hi = _dot_nt(qs[tq:], keys(first_diag + 1)) + bias[tq:]
    z_lo = logits(first_diag)
    stage_logits(0, 0)
    cs_hi = _suffix_sums(_softplus2_operand(z_hi, triangle), tri)
    sp_lo = jnp.concatenate([_softplus2_operand(z_lo[:tq], triangle),
                             _softplus2_operand(z_lo[tq:], None)], axis=0)
    cs_lo = _suffix_sums(sp_lo, tri)
    stage_logits(1, 1)
    no_total = jnp.zeros((tq, LANES), F32)
    pv_hi = _dot(_weights(z_hi, cs_hi, no_total, triangle), values(first_diag + 1))
    r_hi = jnp.concatenate([no_total, _row_total(cs_hi)], axis=0)
    w_lo = jnp.concatenate([_weights(z_lo[:tq], cs_lo[:tq], r_hi[:tq], triangle),
                            _weights(z_lo[tq:], cs_lo[tq:], r_hi[tq:], None)], axis=0)
    acc_ref[...] = (_dot(w_lo, values(first_diag))
                    + jnp.concatenate([jnp.zeros_like(pv_hi), pv_hi], axis=0))
    r_ref[...] = r_hi + _row_total(cs_lo)
    stage_sums(0, 0)

    @pl.when(n_pairs >= 1)
    def _():
        last = n_pairs - 1
        n_quads = lax.shift_right_logical(last, 2)

        def steps(t, count, ends_at_last=False):
            for d in range(count):
                step(t + d, (1 + d) % 4, more_pairs=not (ends_at_last and d == count - 1))

        def body(u, carry):
            steps(1 + 4 * u, 4)
            return carry

        lax.fori_loop(0, n_quads, body, 0)
        for rem in range(4):
            @pl.when((last & 3) == rem)
            def _(rem=rem):
                steps(1 + 4 * n_quads, rem, ends_at_last=True)
                stage_values(last, rem, rem % 2)

    acc = acc_ref[...]
    first_head = lax.broadcasted_iota(jnp.int32, (tk, 2 * HEAD_DIM), 1) < HEAD_DIM
    for half in range(2):
        rows = acc[2 * half * tk:2 * (half + 1) * tk]
        o_ref[0, half * tk:(half + 1) * tk, :] = jnp.where(
            first_head, rows[:tk], rows[tk:]).astype(o_ref.dtype)


def _prompt_attn(q, k, v, bias2):
    b, s, _ = q.shape
    tq, tk = ATT_Q_TILE, ATT_K_TILE
    assert tq % tk == 0 and s % tq == 0
    kv_spec = pl.BlockSpec((1, s, 2 * HEAD_DIM), lambda bb, p, i: (bb, 0, p))
    return pl.pallas_call(
        _prompt_attn_kernel,
        grid=(b, N_HEADS // 2, s // tq),
        in_specs=[pl.BlockSpec(memory_space=pltpu.SMEM),
                  pl.BlockSpec((1, tq, 2 * HEAD_DIM), lambda bb, p, i: (bb, i, p)),
                  kv_spec, kv_spec,
                  _const_spec((tk, tk))],
        out_specs=pl.BlockSpec((1, tq, 2 * HEAD_DIM), lambda bb, p, i: (bb, i, p)),
        out_shape=jax.ShapeDtypeStruct((b, s, D_ATT), BF16),
        scratch_shapes=[pltpu.VMEM((2 * tq, 2 * HEAD_DIM), F32),
                        pltpu.VMEM((2 * tq, LANES), F32),
                        pltpu.VMEM((4, 2, 2 * tq, tk), F32),
                        pltpu.VMEM((2, 2, 2 * tq, tk), F32),
                        pltpu.VMEM((2, 2, 2 * tq, LANES), F32)],
        compiler_params=pltpu.CompilerParams(
            dimension_semantics=("arbitrary", "arbitrary", "arbitrary"),
            vmem_limit_bytes=VMEM_LIMIT_BYTES),
        name="prompt_attn",
    )(bias2, q, k, v, _tri(tk))


def _sample_attn_kernel(pt_ref, qbd_ref, knt_ref, vnt_ref, tri_ref, bias_ref, *refs):
    del pt_ref
    npg = SAMPLE_PAGES_PER_STEP
    k_refs, v_refs = refs[:npg], refs[npg:2 * npg]
    o_ref, acc_ref, r_ref = refs[2 * npg:]
    n = pl.program_id(1)
    hq = qbd_ref.shape[1]
    nq = hq // N_HEADS
    qbd = qbd_ref[0]
    bias = bias_ref[...]
    tri = tri_ref[...]

    def step(pages, mask):
        logits_fns = [lambda kt=kt: _dot(qbd, kt()) + bias for kt, _ in pages]
        value_fns = [lambda w, vt=vt: _dot_nt(w, vt()) for _, vt in pages]
        _stick_breaking_step(logits_fns, value_fns, tri, [mask] * len(pages), acc_ref, r_ref)

    @pl.when(n == 0)
    def _():
        acc_ref[...] = jnp.zeros_like(acc_ref)
        r_ref[...] = jnp.zeros_like(r_ref)
        kpos = lax.broadcasted_iota(jnp.int32, (hq, PAGE_SIZE), 1)
        qpos = lax.broadcasted_iota(jnp.int32, (hq, PAGE_SIZE), 0) % nq
        step([(lambda: knt_ref[0], lambda: vnt_ref[0])], kpos < qpos)

    def page(ref):
        return lambda: ref[0, 0].reshape(N_HEADS * HEAD_DIM, PAGE_SIZE)

    step([(page(k), page(v)) for k, v in zip(k_refs, v_refs)], None)

    @pl.when(n == pl.num_programs(1) - 1)
    def _():
        acc = acc_ref[...]
        for h in range(N_HEADS):
            o_ref[0, h * nq:(h + 1) * nq, :] = acc[h * nq:(h + 1) * nq,
                                                   h * HEAD_DIM:(h + 1) * HEAD_DIM]


def _sample_attn(page_table, qbd, knt, vnt, cache_kt, cache_vt, bias_col):
    bd, n_pages = page_table.shape
    hq = qbd.shape[1]
    npg = SAMPLE_PAGES_PER_STEP
    assert n_pages % npg == 0

    def page_spec(p):
        return pl.BlockSpec(
            (1, 1, N_HEADS, HEAD_DIM, PAGE_SIZE),
            lambda b, n, pt: (0, pt[b, n_pages - 1 - (n * npg + p)], 0, 0, 0))

    per_req = lambda shape: pl.BlockSpec((1,) + shape, lambda b, n, pt: (b, 0, 0))
    const = lambda shape: pl.BlockSpec(shape, lambda b, n, pt: (0, 0))
    pages = [page_spec(p) for p in range(npg)]
    grid_spec = pltpu.PrefetchScalarGridSpec(
        num_scalar_prefetch=1,
        grid=(bd, n_pages // npg),
        in_specs=[per_req((hq, D_ATT)), per_req((D_ATT, PAGE_SIZE)), per_req((D_ATT, PAGE_SIZE)),
                  const((PAGE_SIZE, PAGE_SIZE)), const((hq, 1))] + pages + pages,
        out_specs=per_req((hq, HEAD_DIM)),
        scratch_shapes=[pltpu.VMEM((hq, D_ATT), F32), pltpu.VMEM((hq, LANES), F32)],
    )
    return pl.pallas_call(
        _sample_attn_kernel,
        grid_spec=grid_spec,
        out_shape=jax.ShapeDtypeStruct((bd, hq, HEAD_DIM), F32),
        compiler_params=pltpu.CompilerParams(
            dimension_semantics=("arbitrary", "arbitrary"), vmem_limit_bytes=VMEM_LIMIT_BYTES),
        name="sample_attn",
    )(page_table, qbd, knt, vnt, _tri(PAGE_SIZE), bias_col,
      *([cache_kt] * npg), *([cache_vt] * npg))


def _output_kernel(x_ref, o_ref, ma_ref, gb_ref, wpa_ref, wout_ref, n2_ref, wup_ref, wdown_ref,
                   nf_ref, y_ref):
    y_att = _dot_halves(o_ref[...], wpa_ref[...])
    merged = ma_ref[...] + gb_ref[...] * y_att
    x1 = x_ref[...] + _dot_halves(merged.astype(BF16), wout_ref[...])
    h2 = _rmsnorm(x1, n2_ref[...]).astype(BF16)
    x2 = x1
    ff_chunk = D_MODEL
    for c in range(0, D_FF, ff_chunk):
        f = jnp.maximum(_dot_halves(h2, wup_ref[:, c:c + ff_chunk]), 0.0)
        x2 = x2 + _dot_halves((f * f).astype(BF16), wdown_ref[c:c + ff_chunk, :])
    y_ref[...] = _rmsnorm(x2, nf_ref[...])


def _output(x, o, ma, gb, wpa, wout, n2, wup, wdown, nf, tm):
    n = x.shape[0]
    row = lambda d: pl.BlockSpec((tm, d), lambda i: (i, 0))
    return pl.pallas_call(
        _output_kernel,
        grid=(n // tm,),
        in_specs=[row(D_MODEL), row(D_ATT), row(D_MODEL), row(D_MODEL),
                  _const_spec((D_ATT, D_MODEL)), _const_spec((D_MODEL, D_MODEL)),
                  _const_spec((1, D_MODEL)), _const_spec((D_MODEL, D_FF)),
                  _const_spec((D_FF, D_MODEL)), _const_spec((1, D_MODEL))],
        out_specs=row(D_MODEL),
        out_shape=jax.ShapeDtypeStruct((n, D_MODEL), F32),
        compiler_params=pltpu.CompilerParams(
            dimension_semantics=("arbitrary",), vmem_limit_bytes=VMEM_LIMIT_BYTES),
        name="output_mlp",
    )(x, o, ma, gb, wpa, wout, n2, wup, wdown, nf)


def kernel(x_prompt, x_sample, cache_k, cache_v, state_conv, page_table, norm1_w, w_in, conv_w,
           w_proj_conv, w_proj_attn, att_bias, w_out, norm2_w, w_up, w_down, norm_f_w):
    depth = w_in.shape[0]
    assert depth == 1, "single-layer stack"
    bp, s, _ = x_prompt.shape
    bd, ls, _ = x_sample.shape

    n1 = norm1_w[0][None, :]
    n2 = norm2_w[0][None, :]
    nf = norm_f_w[None, :]
    win = w_in[0].astype(BF16)
    wpc = w_proj_conv[0].astype(BF16)
    wpa = w_proj_attn[0].astype(BF16)
    wout = w_out[0].astype(BF16)
    wup = w_up[0].astype(BF16)
    wdown = w_down[0].astype(BF16)
    convw = conv_w[0]
    bias2 = att_bias[0] * LOG2E

    xp = x_prompt.reshape(bp * s, D_MODEL)
    qp, kbp, vbp, kp, vp, cup, map_, gbp = _inproj_prompt(xp, n1, win, convw, wpc, s)
    op = _prompt_attn(qp.reshape(bp, s, D_ATT), kbp.reshape(bp, s, D_ATT),
                      vbp.reshape(bp, s, D_ATT), bias2)
    yp = _output(xp, op.reshape(bp * s, D_ATT), map_, gbp, wpa, wout, n2, wup, wdown, nf,
                 TOKEN_TILE)

    xs = x_sample.reshape(bd * ls, D_MODEL)
    st = state_conv[0]
    zeros = jnp.zeros((bd, ls, D_CONV), F32)
    p1 = zeros.at[:, 0].set(st[:, 1]).reshape(bd * ls, D_CONV)
    p2 = zeros.at[:, 0].set(st[:, 0]).at[:, 1].set(st[:, 1]).reshape(bd * ls, D_CONV)
    qs, _, _, ks, vs, cus, mas, gbs = _inproj_sample(xs, n1, win, convw, wpc, p1, p2, ls)
    hq = N_HEADS * ls
    q4 = qs.reshape(bd, ls, N_HEADS, HEAD_DIM).transpose(0, 2, 1, 3)
    eye = jnp.eye(N_HEADS, dtype=BF16)
    qbd = (q4[:, :, :, None, :] * eye[None, :, None, :, None]).reshape(bd, hq, D_ATT)
    bias_col = jnp.repeat(bias2, ls)[:, None]
    pad = ((0, 0), (0, 0), (0, PAGE_SIZE - ls))
    knt = jnp.pad(ks.reshape(bd, ls, D_ATT).transpose(0, 2, 1), pad)
    vnt = jnp.pad(vs.reshape(bd, ls, D_ATT).transpose(0, 2, 1), pad)
    cache_kt = cache_k.transpose(0, 1, 3, 4, 2)
    cache_vt = cache_v.transpose(0, 1, 3, 4, 2)
    os_ = _sample_attn(page_table, qbd, knt, vnt, cache_kt, cache_vt, bias_col)
    os_ = os_.reshape(bd, N_HEADS, ls, HEAD_DIM).transpose(0, 2, 1, 3).reshape(bd * ls, D_ATT)
    ys = _output(xs, os_.astype(BF16), mas, gbs, wpa, wout, n2, wup, wdown, nf, bd * ls)

    def paged(x):
        x = x.reshape(depth, bp, s // PAGE_SIZE, N_HEADS, HEAD_DIM, PAGE_SIZE)
        return x.transpose(0, 1, 2, 5, 3, 4)

    return (
        yp.reshape(bp, s, D_MODEL),
        ys.reshape(bd, ls, D_MODEL),
        paged(kp),
        paged(vp),
        cup.reshape(bp, s, D_CONV)[None, :, s - 2:, :],
        ks.reshape(depth, bd, ls, N_HEADS, HEAD_DIM),
        vs.reshape(depth, bd, ls, N_HEADS, HEAD_DIM),
        cus.reshape(bd, ls, D_CONV)[None, :, ls - 2:, :],
    )
```

```python
import functools
import math

import jax
import jax.numpy as jnp
from jax import lax
from jax.experimental import pallas as pl
from jax.experimental.pallas import tpu as pltpu

D_MODEL = 1024
D_CONV = 512
N_HEADS = 8
HEAD_DIM = 64
D_ATT = N_HEADS * HEAD_DIM
D_FF = 4 * D_MODEL
D_IN = 3 * D_CONV + 3 * D_ATT + 2 * D_MODEL
PAGE_SIZE = 128
LANES = 128
RMS_EPS = 1e-6
LOG2E = math.log2(math.e)

VMEM_LIMIT_BYTES = 56 * 1024 * 1024

TOKEN_TILE = 512
ATT_Q_TILE = 512
ATT_K_TILE = 256
SAMPLE_PAGES_PER_STEP = 32

F32 = jnp.float32
BF16 = jnp.bfloat16


def _dot(a, b):
    return lax.dot_general(a, b, (((1,), (0,)), ((), ())), preferred_element_type=F32)


def _dot_nt(a, b):
    return lax.dot_general(a, b, (((1,), (1,)), ((), ())), preferred_element_type=F32)


def _dot_halves(a, b):
    h = a.shape[0] // 2
    return jnp.concatenate([_dot(a[:h], b), _dot(a[h:], b)], axis=0)


def _rmsnorm(x, w):
    return x * lax.rsqrt(jnp.mean(x * x, axis=-1, keepdims=True) + RMS_EPS) * w


SOFTPLUS2_LINEAR_FROM = 100.0


def _softplus2(z):
    clipped = jnp.minimum(z, SOFTPLUS2_LINEAR_FROM)
    return jnp.maximum(z, jnp.log(1.0 + jnp.exp2(clipped)) * LOG2E)


def _softplus2_operand(z, mask):
    sp = _softplus2(z)
    if mask is not None:
        sp = jnp.where(mask, sp, 0.0)
    return sp


def _suffix_sums(sp, tri):
    return _dot(sp, tri) + sp


def _weights(z, cs, r, mask):
    r_full = jnp.concatenate([r] * (z.shape[1] // LANES), axis=1) if z.shape[1] > LANES else r
    w = jnp.exp2(z - cs - r_full)
    if mask is not None:
        w = jnp.where(mask, w, 0.0)
    return w


def _row_total(cs):
    return jnp.broadcast_to(cs[:, 0:1], (cs.shape[0], LANES))


def _tri(n):
    j = lax.broadcasted_iota(jnp.int32, (n, n), 0)
    s = lax.broadcasted_iota(jnp.int32, (n, n), 1)
    return (j > s).astype(BF16)


def _const_spec(shape):
    return pl.BlockSpec(shape, lambda *_: (0,) * len(shape))


def _inproj_body(x_ref, n1_ref, win_ref, convw_ref, wpc_ref, prev_fn,
                 q_ref, kb_ref, vb_ref, k_ref, v_ref, cu_ref, ma_ref, gb_ref):
    x = x_ref[...]
    hb = _rmsnorm(x, n1_ref[...]).astype(BF16)

    def proj(lo, hi):
        return _dot_halves(hb, win_ref[:, lo:hi])

    o = 0
    b_g = proj(o, o + D_CONV); o += D_CONV
    c_g = proj(o, o + D_CONV); o += D_CONV
    u = proj(o, o + D_CONV); o += D_CONV
    cu = c_g * u
    cu_ref[...] = cu
    prev1, prev2 = prev_fn(cu)
    conv = convw_ref[0:1, :] * prev2 + convw_ref[1:2, :] * prev1 + convw_ref[2:3, :] * cu
    gated = (b_g * conv).astype(BF16)

    q = proj(o, o + D_ATT); o += D_ATT
    q_ref[...] = (q * (LOG2E * HEAD_DIM ** -0.5)).astype(BF16)
    k = proj(o, o + D_ATT); o += D_ATT
    _store_kv(k_ref, k)
    kb_ref[...] = k.astype(BF16)
    v = proj(o, o + D_ATT); o += D_ATT
    _store_kv(v_ref, v)
    vb_ref[...] = v.astype(BF16)
    y_conv = _dot_halves(gated, wpc_ref[...])
    gate_conv = proj(o, o + D_MODEL); o += D_MODEL
    ma_ref[...] = jax.nn.sigmoid(gate_conv) * y_conv
    gate_att = proj(o, o + D_MODEL); o += D_MODEL
    gb_ref[...] = jax.nn.sigmoid(gate_att)


def _inproj_prompt_kernel(tiles_per_seq, x_ref, n1_ref, win_ref, convw_ref, wpc_ref,
                          q_ref, kb_ref, vb_ref, k_ref, v_ref, cu_ref, ma_ref, gb_ref, carry_ref):
    tm = x_ref.shape[0]

    @pl.when(pl.program_id(0) % tiles_per_seq == 0)
    def _():
        carry_ref[...] = jnp.zeros_like(carry_ref)

    def prev_fn(cu):
        row = lax.broadcasted_iota(jnp.int32, cu.shape, 0)
        c1 = carry_ref[7:8, :]
        c2 = carry_ref[6:7, :]
        prev1 = jnp.where(row == 0, c1, pltpu.roll(cu, 1, 0))
        prev2 = jnp.where(row == 0, c2, jnp.where(row == 1, c1, pltpu.roll(cu, 2, 0)))
        carry_ref[...] = cu[tm - 8:, :]
        return prev1, prev2

    _inproj_body(x_ref, n1_ref, win_ref, convw_ref, wpc_ref, prev_fn,
                 q_ref, kb_ref, vb_ref, k_ref, v_ref, cu_ref, ma_ref, gb_ref)


def _inproj_sample_kernel(seq, x_ref, n1_ref, win_ref, convw_ref, wpc_ref, p1_ref, p2_ref,
                          q_ref, kb_ref, vb_ref, k_ref, v_ref, cu_ref, ma_ref, gb_ref):
    def prev_fn(cu):
        pos = lax.broadcasted_iota(jnp.int32, cu.shape, 0) % seq
        prev1 = jnp.where(pos == 0, p1_ref[...], pltpu.roll(cu, 1, 0))
        prev2 = jnp.where(pos < 2, p2_ref[...], pltpu.roll(cu, 2, 0))
        return prev1, prev2

    _inproj_body(x_ref, n1_ref, win_ref, convw_ref, wpc_ref, prev_fn,
                 q_ref, kb_ref, vb_ref, k_ref, v_ref, cu_ref, ma_ref, gb_ref)


def _store_kv(ref, x):
    if len(ref.shape) == 2:
        ref[...] = x
    else:
        for p in range(ref.shape[0]):
            ref[p] = x[p * PAGE_SIZE:(p + 1) * PAGE_SIZE, :].T


def _inproj_out(n, tm, paged_kv):
    row = lambda d: pl.BlockSpec((tm, d), lambda i: (i, 0))
    if paged_kv:
        kv_shape = jax.ShapeDtypeStruct((n // PAGE_SIZE, D_ATT, PAGE_SIZE), F32)
        kv_spec = pl.BlockSpec((tm // PAGE_SIZE, D_ATT, PAGE_SIZE), lambda i: (i, 0, 0))
    else:
        kv_shape = jax.ShapeDtypeStruct((n, D_ATT), F32)
        kv_spec = row(D_ATT)
    shapes = [
        jax.ShapeDtypeStruct((n, D_ATT), BF16),
        jax.ShapeDtypeStruct((n, D_ATT), BF16),
        jax.ShapeDtypeStruct((n, D_ATT), BF16),
        kv_shape,
        kv_shape,
        jax.ShapeDtypeStruct((n, D_CONV), F32),
        jax.ShapeDtypeStruct((n, D_MODEL), F32),
        jax.ShapeDtypeStruct((n, D_MODEL), F32),
    ]
    specs = [row(D_ATT)] * 3 + [kv_spec] * 2 + [row(D_CONV), row(D_MODEL), row(D_MODEL)]
    return shapes, specs


def _inproj_weight_specs():
    return [_const_spec((1, D_MODEL)), _const_spec((D_MODEL, D_IN)),
            _const_spec((3, D_CONV)), _const_spec((D_CONV, D_MODEL))]


def _inproj_prompt(x, n1, win, convw, wpc, seq_len):
    n = x.shape[0]
    tm = TOKEN_TILE
    shapes, specs = _inproj_out(n, tm, paged_kv=True)
    return pl.pallas_call(
        functools.partial(_inproj_prompt_kernel, seq_len // tm),
        grid=(n // tm,),
        in_specs=[pl.BlockSpec((tm, D_MODEL), lambda i: (i, 0))] + _inproj_weight_specs(),
        out_specs=specs,
        out_shape=shapes,
        scratch_shapes=[pltpu.VMEM((8, D_CONV), F32)],
        compiler_params=pltpu.CompilerParams(
            dimension_semantics=("arbitrary",), vmem_limit_bytes=VMEM_LIMIT_BYTES),
        name="inproj_prompt",
    )(x, n1, win, convw, wpc)


def _inproj_sample(x, n1, win, convw, wpc, p1, p2, seq):
    n = x.shape[0]
    shapes, specs = _inproj_out(n, n, paged_kv=False)
    full = lambda d: pl.BlockSpec((n, d), lambda i: (0, 0))
    return pl.pallas_call(
        functools.partial(_inproj_sample_kernel, seq),
        grid=(1,),
        in_specs=[full(D_MODEL)] + _inproj_weight_specs() + [full(D_CONV), full(D_CONV)],
        out_specs=specs,
        out_shape=shapes,
        compiler_params=pltpu.CompilerParams(
            dimension_semantics=("arbitrary",), vmem_limit_bytes=VMEM_LIMIT_BYTES),
        name="inproj_sample",
    )(x, n1, win, convw, wpc, p1, p2)


def _stick_breaking_step(logits_fns, value_fns, tri, masks, acc_ref, r_ref):
    zs = [f() for f in logits_fns]
    sums = [_suffix_sums(_softplus2_operand(z, m), tri) for z, m in zip(zs, masks)]
    r = r_ref[...]
    acc = acc_ref[...]
    for z, cs, m, value_fn in zip(zs, sums, masks, value_fns):
        acc = acc + value_fn(_weights(z, cs, r, m))
        r = r + _row_total(cs)
    acc_ref[...] = acc
    r_ref[...] = r


def _prompt_attn_kernel(bias_ref, q_ref, k_ref, v_ref, tri_ref, o_ref,
                        acc_ref, r_ref, z_buf, c_buf, t_buf):
    tq, tk = ATT_Q_TILE, ATT_K_TILE
    pair = pl.program_id(1)
    i = pl.program_id(2)

    assert tq == 2 * tk
    qq = q_ref[0]
    lane = lax.broadcasted_iota(jnp.int32, qq.shape, 1)
    zero = jnp.zeros_like(qq)
    qa = jnp.where(lane < HEAD_DIM, qq, zero)
    qb = jnp.where(lane >= HEAD_DIM, qq, zero)
    qs = jnp.concatenate([qa[:tk], qb[:tk], qa[tk:], qb[tk:]], axis=0)
    row = lax.broadcasted_iota(jnp.int32, (2 * tq, 1), 0)
    bias = jnp.where((row // tk) % 2 == 0, bias_ref[2 * pair], bias_ref[2 * pair + 1])
    tri = tri_ref[...]

    def keys(j):
        return k_ref[0, pl.ds(pl.multiple_of(j * tk, tk), tk), :]

    def values(j):
        return v_ref[0, pl.ds(pl.multiple_of(j * tk, tk), tk), :]

    def logits(j):
        return _dot_nt(qs, keys(j)) + bias

    first_diag = 2 * i
    n_pairs = i
    top = first_diag - 1

    def pair(m):
        return [top - 2 * m - sub for sub in range(2)]

    def stage_logits(m, zslot):
        for sub, j in enumerate(pair(m)):
            z_buf[zslot, sub] = logits(jnp.maximum(j, 0))

    def stage_sums(zslot, cslot):
        for sub in range(2):
            cs = _suffix_sums(_softplus2_operand(z_buf[zslot, sub], None), tri)
            c_buf[cslot, sub] = cs
            t_buf[cslot, sub] = _row_total(cs)

    def stage_values(m, zslot, cslot):
        r = r_ref[...]
        acc = acc_ref[...]
        for sub, j in enumerate(pair(m)):
            w = _weights(z_buf[zslot, sub], c_buf[cslot, sub], r, None)
            acc = acc + _dot(w, values(j))
            r = r + t_buf[cslot, sub]
        acc_ref[...] = acc
        r_ref[...] = r

    def step(t, phase, more_pairs=True):
        za, zb, zc = (phase + 1) % 4, phase, (phase - 1) % 4
        cb, cc = phase % 2, (phase - 1) % 2
        r = r_ref[...]
        acc = acc_ref[...]
        for sub in range(2):
            w = _weights(z_buf[zc, sub], c_buf[cc, sub], r, None)
            acc = acc + _dot(w, values(pair(t - 1)[sub]))
            r = r + t_buf[cc, sub]
            cs = _suffix_sums(_softplus2_operand(z_buf[zb, sub], None), tri)
            c_buf[cb, sub] = cs
            t_buf[cb, sub] = _row_total(cs)
            if more_pairs:
                z_buf[za, sub] = logits(jnp.maximum(pair(t + 1)[sub], 0))
        acc_ref[...] = acc
        r_ref[...] = r

    triangle = (lax.broadcasted_iota(jnp.int32, (tq, tk), 1)
                < lax.broadcasted_iota(jnp.int32, (tq, tk), 0) % tk)
    z_hi = _dot_nt(qs[tq:], keys(first_diag + 1)) + bias[tq:]
    z_lo = logits(first_diag)
    stage_logits(0, 0)
    cs_hi = _suffix_sums(_softplus2_operand(z_hi, triangle), tri)
    sp_lo = jnp.concatenate([_softplus2_operand(z_lo[:tq], triangle),
                             _softplus2_operand(z_lo[tq:], None)], axis=0)
    cs_lo = _suffix_sums(sp_lo, tri)
    stage_logits(1, 1)
    no_total = jnp.zeros((tq, LANES), F32)
    pv_hi = _dot(_weights(z_hi, cs_hi, no_total, triangle), values(first_diag + 1))
    r_hi = jnp.concatenate([no_total, _row_total(cs_hi)], axis=0)
    w_lo = jnp.concatenate([_weights(z_lo[:tq], cs_lo[:tq], r_hi[:tq], triangle),
                            _weights(z_lo[tq:], cs_lo[tq:], r_hi[tq:], None)], axis=0)
    acc_ref[...] = (_dot(w_lo, values(first_diag))
                    + jnp.concatenate([jnp.zeros_like(pv_hi), pv_hi], axis=0))
    r_ref[...] = r_hi + _row_total(cs_lo)
    stage_sums(0, 0)

    @pl.when(n_pairs >= 1)
    def _():
        last = n_pairs - 1
        n_quads = lax.shift_right_logical(last, 2)

        def steps(t, count, ends_at_last=False):
            for d in range(count):
                step(t + d, (1 + d) % 4, more_pairs=not (ends_at_last and d == count - 1))

        def body(u, carry):
            steps(1 + 4 * u, 4)
            return carry

        lax.fori_loop(0, n_quads, body, 0)
        for rem in range(4):
            @pl.when((last & 3) == rem)
            def _(rem=rem):
                steps(1 + 4 * n_quads, rem, ends_at_last=True)
                stage_values(last, rem, rem % 2)

    acc = acc_ref[...]
    first_head = lax.broadcasted_iota(jnp.int32, (tk, 2 * HEAD_DIM), 1) < HEAD_DIM
    for half in range(2):
        rows = acc[2 * half * tk:2 * (half + 1) * tk]
        o_ref[0, half * tk:(half + 1) * tk, :] = jnp.where(
            first_head, rows[:tk], rows[tk:]).astype(o_ref.dtype)


def _prompt_attn(q, k, v, bias2):
    b, s, _ = q.shape
    tq, tk = ATT_Q_TILE, ATT_K_TILE
    assert tq % tk == 0 and s % tq == 0
    kv_spec = pl.BlockSpec((1, s, 2 * HEAD_DIM), lambda bb, p, i: (bb, 0, p))
    return pl.pallas_call(
        _prompt_attn_kernel,
        grid=(b, N_HEADS // 2, s // tq),
        in_specs=[pl.BlockSpec(memory_space=pltpu.SMEM),
                  pl.BlockSpec((1, tq, 2 * HEAD_DIM), lambda bb, p, i: (bb, i, p)),
                  kv_spec, kv_spec,
                  _const_spec((tk, tk))],
        out_specs=pl.BlockSpec((1, tq, 2 * HEAD_DIM), lambda bb, p, i: (bb, i, p)),
        out_shape=jax.ShapeDtypeStruct((b, s, D_ATT), BF16),
        scratch_shapes=[pltpu.VMEM((2 * tq, 2 * HEAD_DIM), F32),
                        pltpu.VMEM((2 * tq, LANES), F32),
                        pltpu.VMEM((4, 2, 2 * tq, tk), F32),
                        pltpu.VMEM((2, 2, 2 * tq, tk), F32),
                        pltpu.VMEM((2, 2, 2 * tq, LANES), F32)],
        compiler_params=pltpu.CompilerParams(
            dimension_semantics=("arbitrary", "arbitrary", "arbitrary"),
            vmem_limit_bytes=VMEM_LIMIT_BYTES),
        name="prompt_attn",
    )(bias2, q, k, v, _tri(tk))


def _sample_attn_kernel(pt_ref, qbd_ref, knt_ref, vnt_ref, tri_ref, bias_ref, *refs):
    del pt_ref
    npg = SAMPLE_PAGES_PER_STEP
    k_refs, v_refs = refs[:npg], refs[npg:2 * npg]
    o_ref, acc_ref, r_ref = refs[2 * npg:]
    n = pl.program_id(1)
    hq = qbd_ref.shape[1]
    nq = hq // N_HEADS
    qbd = qbd_ref[0]
    bias = bias_ref[...]
    tri = tri_ref[...]

    def step(pages, mask):
        logits_fns = [lambda kt=kt: _dot(qbd, kt()) + bias for kt, _ in pages]
        value_fns = [lambda w, vt=vt: _dot_nt(w, vt()) for _, vt in pages]
        _stick_breaking_step(logits_fns, value_fns, tri, [mask] * len(pages), acc_ref, r_ref)

    @pl.when(n == 0)
    def _():
        acc_ref[...] = jnp.zeros_like(acc_ref)
        r_ref[...] = jnp.zeros_like(r_ref)
        kpos = lax.broadcasted_iota(jnp.int32, (hq, PAGE_SIZE), 1)
        qpos = lax.broadcasted_iota(jnp.int32, (hq, PAGE_SIZE), 0) % nq
        step([(lambda: knt_ref[0], lambda: vnt_ref[0])], kpos < qpos)

    def page(ref):
        return lambda: ref[0, 0].reshape(N_HEADS * HEAD_DIM, PAGE_SIZE)

    step([(page(k), page(v)) for k, v in zip(k_refs, v_refs)], None)

    @pl.when(n == pl.num_programs(1) - 1)
    def _():
        acc = acc_ref[...]
        for h in range(N_HEADS):
            o_ref[0, h * nq:(h + 1) * nq, :] = acc[h * nq:(h + 1) * nq,
                                                   h * HEAD_DIM:(h + 1) * HEAD_DIM]


def _sample_attn(page_table, qbd, knt, vnt, cache_kt, cache_vt, bias_col):
    bd, n_pages = page_table.shape
    hq = qbd.shape[1]
    npg = SAMPLE_PAGES_PER_STEP
    assert n_pages % npg == 0

    def page_spec(p):
        return pl.BlockSpec(
            (1, 1, N_HEADS, HEAD_DIM, PAGE_SIZE),
            lambda b, n, pt: (0, pt[b, n_pages - 1 - (n * npg + p)], 0, 0, 0))

    per_req = lambda shape: pl.BlockSpec((1,) + shape, lambda b, n, pt: (b, 0, 0))
    const = lambda shape: pl.BlockSpec(shape, lambda b, n, pt: (0, 0))
    pages = [page_spec(p) for p in range(npg)]
    grid_spec = pltpu.PrefetchScalarGridSpec(
        num_scalar_prefetch=1,
        grid=(bd, n_pages // npg),
        in_specs=[per_req((hq, D_ATT)), per_req((D_ATT, PAGE_SIZE)), per_req((D_ATT, PAGE_SIZE)),
                  const((PAGE_SIZE, PAGE_SIZE)), const((hq, 1))] + pages + pages,
        out_specs=per_req((hq, HEAD_DIM)),
        scratch_shapes=[pltpu.VMEM((hq, D_ATT), F32), pltpu.VMEM((hq, LANES), F32)],
    )
    return pl.pallas_call(
        _sample_attn_kernel,
        grid_spec=grid_spec,
        out_shape=jax.ShapeDtypeStruct((bd, hq, HEAD_DIM), F32),
        compiler_params=pltpu.CompilerParams(
            dimension_semantics=("arbitrary", "arbitrary"), vmem_limit_bytes=VMEM_LIMIT_BYTES),
        name="sample_attn",
    )(page_table, qbd, knt, vnt, _tri(PAGE_SIZE), bias_col,
      *([cache_kt] * npg), *([cache_vt] * npg))


def _output_kernel(x_ref, o_ref, ma_ref, gb_ref, wpa_ref, wout_ref, n2_ref, wup_ref, wdown_ref,
                   nf_ref, y_ref):
    y_att = _dot_halves(o_ref[...], wpa_ref[...])
    merged = ma_ref[...] + gb_ref[...] * y_att
    x1 = x_ref[...] + _dot_halves(merged.astype(BF16), wout_ref[...])
    h2 = _rmsnorm(x1, n2_ref[...]).astype(BF16)
    x2 = x1
    ff_chunk = D_MODEL
    for c in range(0, D_FF, ff_chunk):
        f = jnp.maximum(_dot_halves(h2, wup_ref[:, c:c + ff_chunk]), 0.0)
        x2 = x2 + _dot_halves((f * f).astype(BF16), wdown_ref[c:c + ff_chunk, :])
    y_ref[...] = _rmsnorm(x2, nf_ref[...])


def _output(x, o, ma, gb, wpa, wout, n2, wup, wdown, nf, tm):
    n = x.shape[0]
    row = lambda d: pl.BlockSpec((tm, d), lambda i: (i, 0))
    return pl.pallas_call(
        _output_kernel,
        grid=(n // tm,),
        in_specs=[row(D_MODEL), row(D_ATT), row(D_MODEL), row(D_MODEL),
                  _const_spec((D_ATT, D_MODEL)), _const_spec((D_MODEL, D_MODEL)),
                  _const_spec((1, D_MODEL)), _const_spec((D_MODEL, D_FF)),
                  _const_spec((D_FF, D_MODEL)), _const_spec((1, D_MODEL))],
        out_specs=row(D_MODEL),
        out_shape=jax.ShapeDtypeStruct((n, D_MODEL), F32),
        compiler_params=pltpu.CompilerParams(
            dimension_semantics=("arbitrary",), vmem_limit_bytes=VMEM_LIMIT_BYTES),
        name="output_mlp",
    )(x, o, ma, gb, wpa, wout, n2, wup, wdown, nf)


def kernel(x_prompt, x_sample, cache_k, cache_v, state_conv, page_table, norm1_w, w_in, conv_w,
           w_proj_conv, w_proj_attn, att_bias, w_out, norm2_w, w_up, w_down, norm_f_w):
    depth = w_in.shape[0]
    assert depth == 1, "single-layer stack"
    bp, s, _ = x_prompt.shape
    bd, ls, _ = x_sample.shape

    n1 = norm1_w[0][None, :]
    n2 = norm2_w[0][None, :]
    nf = norm_f_w[None, :]
    win = w_in[0].astype(BF16)
    wpc = w_proj_conv[0].astype(BF16)
    wpa = w_proj_attn[0].astype(BF16)
    wout = w_out[0].astype(BF16)
    wup = w_up[0].astype(BF16)
    wdown = w_down[0].astype(BF16)
    convw = conv_w[0]
    bias2 = att_bias[0] * LOG2E

    xp = x_prompt.reshape(bp * s, D_MODEL)
    qp, kbp, vbp, kp, vp, cup, map_, gbp = _inproj_prompt(xp, n1, win, convw, wpc, s)
    op = _prompt_attn(qp.reshape(bp, s, D_ATT), kbp.reshape(bp, s, D_ATT),
                      vbp.reshape(bp, s, D_ATT), bias2)
    yp = _output(xp, op.reshape(bp * s, D_ATT), map_, gbp, wpa, wout, n2, wup, wdown, nf,
                 TOKEN_TILE)

    xs = x_sample.reshape(bd * ls, D_MODEL)
    st = state_conv[0]
    zeros = jnp.zeros((bd, ls, D_CONV), F32)
    p1 = zeros.at[:, 0].set(st[:, 1]).reshape(bd * ls, D_CONV)
    p2 = zeros.at[:, 0].set(st[:, 0]).at[:, 1].set(st[:, 1]).reshape(bd * ls, D_CONV)
    qs, _, _, ks, vs, cus, mas, gbs = _inproj_sample(xs, n1, win, convw, wpc, p1, p2, ls)
    hq = N_HEADS * ls
    q4 = qs.reshape(bd, ls, N_HEADS, HEAD_DIM).transpose(0, 2, 1, 3)
    eye = jnp.eye(N_HEADS, dtype=BF16)
    qbd = (q4[:, :, :, None, :] * eye[None, :, None, :, None]).reshape(bd, hq, D_ATT)
    bias_col = jnp.repeat(bias2, ls)[:, None]
    pad = ((0, 0), (0, 0), (0, PAGE_SIZE - ls))
    knt = jnp.pad(ks.reshape(bd, ls, D_ATT).transpose(0, 2, 1), pad)
    vnt = jnp.pad(vs.reshape(bd, ls, D_ATT).transpose(0, 2, 1), pad)
    cache_kt = cache_k.transpose(0, 1, 3, 4, 2)
    cache_vt = cache_v.transpose(0, 1, 3, 4, 2)
    os_ = _sample_attn(page_table, qbd, knt, vnt, cache_kt, cache_vt, bias_col)
    os_ = os_.reshape(bd, N_HEADS, ls, HEAD_DIM).transpose(0, 2, 1, 3).reshape(bd * ls, D_ATT)
    ys = _output(xs, os_.astype(BF16), mas, gbs, wpa, wout, n2, wup, wdown, nf, bd * ls)

    def paged(x):
        x = x.reshape(depth, bp, s // PAGE_SIZE, N_HEADS, HEAD_DIM, PAGE_SIZE)
        return x.transpose(0, 1, 2, 5, 3, 4)

    return (
        yp.reshape(bp, s, D_MODEL),
        ys.reshape(bd, ls, D_MODEL),
        paged(kp),
        paged(vp),
        cup.reshape(bp, s, D_CONV)[None, :, s - 2:, :],
        ks.reshape(depth, bd, ls, N_HEADS, HEAD_DIM),
        vs.reshape(depth, bd, ls, N_HEADS, HEAD_DIM),
        cus.reshape(bd, ls, D_CONV)[None, :, ls - 2:, :],
    )
```
